```python
import jax, jax.numpy as jnp
from jax import lax
import numpy as np

D_MODEL = 2048
BATCH = 16
SEQ = 2048
DEPTH = 2

CTX_LEN = 256
GRID_W = 64
MIX_WIDTH = D_MODEL
ATTN_WIDTH = MIX_WIDTH // 2
N_HEADS = 8
HEAD_DIM = ATTN_WIDTH // N_HEADS
WIN_ROWS = 8
WIN_COLS = 16
POOL_WIDTH = MIX_WIDTH - ATTN_WIDTH
POOL_WINDOWS = (2, 4, 8, 16)
N_POOL_GROUPS = len(POOL_WINDOWS)
POOL_GROUP = POOL_WIDTH // N_POOL_GROUPS
IN_WIDTH = 3 * ATTN_WIDTH + POOL_WIDTH
N_EXPERTS = 32
N_EXPERT_GROUPS = 8
EXPERTS_PER_GROUP = N_EXPERTS // N_EXPERT_GROUPS
TOP_K = 2
EXPERT_FF = D_MODEL // 2
EXPERT_BLOCK = 128
N_MOD = 6
EPS = 1e-6

kernel_name = 'hymba_style_natten_pool_groupmoe_dit'


def rmsnorm(x, g):
    xf = x.astype(jnp.float32)
    y = xf * lax.rsqrt(jnp.mean(xf * xf, axis=-1, keepdims=True) + EPS)
    return (y * g.astype(jnp.float32)).astype(x.dtype)


def split_heads(t):
    B, L, _ = t.shape
    return t.reshape(B, L, N_HEADS, HEAD_DIM).transpose(0, 2, 1, 3)


def merge_heads(t):
    B, H, L, dh = t.shape
    return t.transpose(0, 2, 1, 3).reshape(B, L, H * dh)


def neighbourhood_attention(q, k, v, kc, vc, rpb):
    B, H, L, dh = q.shape
    rows = L // GRID_W
    kr = min(WIN_ROWS, rows)
    kw = WIN_COLS
    r_start = jnp.clip(jnp.arange(rows) - kr // 2, 0, rows - kr)
    qcol = jnp.arange(GRID_W)
    c_start = jnp.clip(qcol - kw // 2, 0, GRID_W - kw)
    kcol = jnp.arange(GRID_W)
    col_mask = (kcol[None, :] >= c_start[:, None]) & (kcol[None, :] < c_start[:, None] + kw)
    dc_idx = jnp.clip(kcol[None, :] - qcol[:, None] + kw - 1, 0, 2 * kw - 2)
    kg = k.reshape(B, H, rows, GRID_W, dh)
    vg = v.reshape(B, H, rows, GRID_W, dh)
    q_rows = jnp.moveaxis(q.reshape(B, H, rows, GRID_W, dh), 2, 0)
    n_loc = kr * GRID_W

    def one_row(args):
        q_r, r, r0 = args
        kb = lax.dynamic_slice_in_dim(kg, r0, kr, axis=2)
        vb = lax.dynamic_slice_in_dim(vg, r0, kr, axis=2)
        dr_idx = r0 + jnp.arange(kr) - r + WIN_ROWS - 1
        bias = rpb[:, dr_idx[None, :, None], dc_idx[:, None, :]]
        s_loc = jnp.einsum('bhqd,bhrkd->bhqrk', q_r, kb).astype(jnp.float32) + bias.astype(jnp.float32)
        s_loc = jnp.where(col_mask[:, None, :], s_loc, -jnp.inf).reshape(B, H, GRID_W, n_loc)
        s_ctx = jnp.einsum('bhqd,bhkd->bhqk', q_r, kc).astype(jnp.float32)
        p = jax.nn.softmax(jnp.concatenate([s_loc, s_ctx], axis=-1), axis=-1).astype(v.dtype)
        p_loc = p[..., :n_loc].reshape(B, H, GRID_W, kr, GRID_W)
        return (jnp.einsum('bhqrk,bhrkd->bhqd', p_loc, vb)
                + jnp.einsum('bhqk,bhkd->bhqd', p[..., n_loc:], vc))

    out = lax.map(one_row, (q_rows, jnp.arange(rows), r_start))
    return jnp.moveaxis(out, 0, 2).reshape(B, H, L, dh)


def context_attention(qc, kc, vc):
    s = jnp.einsum('bhqd,bhkd->bhqk', qc, kc).astype(jnp.float32)
    p = jax.nn.softmax(s, axis=-1).astype(vc.dtype)
    return jnp.einsum('bhqk,bhkd->bhqd', p, vc)


def multiscale_pool(u, w_pool, pool_scale):
    B, L, _ = u.shape
    uf = u.reshape(B, L, N_POOL_GROUPS, POOL_GROUP).astype(jnp.float32)
    csum = jnp.concatenate([jnp.zeros((B, 1, N_POOL_GROUPS, POOL_GROUP), jnp.float32),
                            jnp.cumsum(uf, axis=1)], axis=1)
    t = jnp.arange(L)
    outs = []
    for gi, w in enumerate(POOL_WINDOWS):
        left = w // 2
        right = w - 1 - left
        a = jnp.clip(t - left, 0, L)
        b = jnp.clip(t + right + 1, 0, L)
        mean = (csum[:, b, gi] - csum[:, a, gi]) / (b - a).astype(jnp.float32)[None, :, None]
        outs.append(mean - uf[:, :, gi])
    pooled = jnp.stack(outs, axis=2).astype(u.dtype)
    y = jnp.einsum('blgc,gcd->blgd', pooled, w_pool)
    return y.reshape(B, L, POOL_WIDTH) * pool_scale


def grouped_moe(h, w_router, router_bias, w_gate, w_up, w_down):
    N, D = h.shape
    s = jax.nn.sigmoid((h @ w_router).astype(jnp.float32))
    sb = (s + router_bias.astype(jnp.float32)).reshape(N, N_EXPERT_GROUPS, EXPERTS_PER_GROUP)
    grp_score = lax.top_k(sb, 2)[0].sum(axis=-1)
    g = jnp.argmax(grp_score, axis=-1)
    sb_g = jnp.take_along_axis(sb, g[:, None, None], axis=1)[:, 0]
    _, loc = lax.top_k(sb_g, TOP_K)
    expert = g[:, None] * EXPERTS_PER_GROUP + loc
    gates = jnp.take_along_axis(s, expert, axis=1)
    gates = (gates / jnp.sum(gates, axis=-1, keepdims=True)).astype(h.dtype)

    A = N * TOP_K
    n_blk = (A + N_EXPERTS * (EXPERT_BLOCK - 1)) // EXPERT_BLOCK + 1
    P = n_blk * EXPERT_BLOCK
    flat_e = expert.reshape(-1)
    order = jnp.argsort(flat_e)
    sorted_e = flat_e[order]
    counts = jnp.bincount(flat_e, length=N_EXPERTS)
    starts = jnp.cumsum(counts) - counts
    padded = ((counts + EXPERT_BLOCK - 1) // EXPERT_BLOCK) * EXPERT_BLOCK
    pend = jnp.cumsum(padded)
    pstarts = pend - padded
    dest_sorted = pstarts[sorted_e] + jnp.arange(A) - starts[sorted_e]
    dest = jnp.zeros((A,), jnp.int32).at[order].set(dest_sorted.astype(jnp.int32))
    row_src = jnp.full((P,), N, jnp.int32).at[dest].set(jnp.arange(A, dtype=jnp.int32) // TOP_K)
    h_pad = jnp.concatenate([h, jnp.zeros((1, D), h.dtype)], axis=0)
    xb = h_pad[row_src].reshape(n_blk, EXPERT_BLOCK, D)
    blk_start = jnp.arange(n_blk) * EXPERT_BLOCK
    blk_e = jnp.minimum(jnp.sum(blk_start[:, None] >= pend[None, :], axis=1), N_EXPERTS - 1)

    def expert_block(args):
        xb_i, e = args
        return (jax.nn.silu(xb_i @ w_gate[e]) * (xb_i @ w_up[e])) @ w_down[e]

    yb = lax.map(expert_block, (xb, blk_e)).reshape(P, D)
    y_rows = yb[dest].reshape(N, TOP_K, D)
    return jnp.einsum('nk,nkd->nd', gates, y_rows)


def setup_inputs(seed: int = 0) -> dict:
    key = jax.random.key(seed)
    ks = jax.random.split(key, 19)
    nrm = lambda k, shape, scale: jax.random.normal(k, shape, jnp.float32) * scale
    D = D_MODEL
    return {
        'x': nrm(ks[0], (BATCH, SEQ, D), 1.0),
        'c': nrm(ks[1], (BATCH, D), 1.0),
        'ctx': nrm(ks[2], (BATCH, CTX_LEN, D), 1.0),
        'c_ctx': nrm(ks[3], (D,), 1.0),
        'w_mod': nrm(ks[4], (DEPTH, D, N_MOD * D), 0.5 * D ** -0.5),
        'b_mod': nrm(ks[5], (DEPTH, N_MOD * D), 0.02),
        'norm1': 1.0 + nrm(ks[6], (DEPTH, D), 0.1),
        'norm2': 1.0 + nrm(ks[7], (DEPTH, D), 0.1),
        'w_in': nrm(ks[8], (DEPTH, D, IN_WIDTH), D ** -0.5),
        'rpb': nrm(ks[9], (DEPTH, N_HEADS, 2 * WIN_ROWS - 1, 2 * WIN_COLS - 1), 0.5),
        'w_pool': nrm(ks[10], (DEPTH, N_POOL_GROUPS, POOL_GROUP, POOL_GROUP), POOL_GROUP ** -0.5),
        'pool_scale': 1.0 + nrm(ks[11], (DEPTH, POOL_WIDTH), 0.1),
        'w_out': nrm(ks[12], (DEPTH, MIX_WIDTH, D), MIX_WIDTH ** -0.5),
        'w_router': nrm(ks[13], (D, N_EXPERTS), D ** -0.5),
        'router_bias': nrm(ks[14], (N_EXPERTS,), 0.01),
        'w_gate': nrm(ks[15], (DEPTH, N_EXPERTS, D, EXPERT_FF), D ** -0.5),
        'w_up': nrm(ks[16], (DEPTH, N_EXPERTS, D, EXPERT_FF), D ** -0.5),
        'w_down': nrm(ks[17], (DEPTH, N_EXPERTS, EXPERT_FF, D), EXPERT_FF ** -0.5),
        'norm_final': 1.0 + nrm(ks[18], (D,), 0.1),
    }


def reference(x, c, ctx, c_ctx, w_mod, b_mod, norm1, norm2, w_in, rpb, w_pool, pool_scale, w_out,
              w_router, router_bias, w_gate, w_up, w_down, norm_final):
    B, L, D = x.shape
    Lc = ctx.shape[1]
    A = ATTN_WIDTH
    q_scale = HEAD_DIM ** -0.5
    xc = ctx
    for l in range(DEPTH):
        last = l == DEPTH - 1
        mod = jax.nn.silu(c) @ w_mod[l] + b_mod[l]
        mod_c = jax.nn.silu(c_ctx) @ w_mod[l] + b_mod[l]
        sh1, sc1, g1, sh2, sc2, g2 = jnp.split(mod[:, None, :], N_MOD, axis=-1)
        csh1, csc1, cg1, csh2, csc2, cg2 = jnp.split(mod_c, N_MOD, axis=-1)

        h = rmsnorm(x, norm1[l]) * (1 + sc1) + sh1
        hc = rmsnorm(xc, norm1[l]) * (1 + csc1) + csh1
        q, k, v, u = jnp.split(h @ w_in[l], [A, 2 * A, 3 * A], axis=-1)
        if last:
            kc, vc = jnp.split(hc @ w_in[l][:, A:3 * A], 2, axis=-1)
        else:
            qc, kc, vc, uc = jnp.split(hc @ w_in[l], [A, 2 * A, 3 * A], axis=-1)
        kch, vch = split_heads(kc), split_heads(vc)
        attn = neighbourhood_attention(split_heads(q) * q_scale, split_heads(k), split_heads(v),
                                       kch, vch, rpb[l])
        pool = multiscale_pool(u, w_pool[l], pool_scale[l])
        x = x + g1 * (jnp.concatenate([merge_heads(attn), pool], axis=-1) @ w_out[l])
        if not last:
            attn_c = context_attention(split_heads(qc) * q_scale, kch, vch)
            pool_c = multiscale_pool(uc, w_pool[l], pool_scale[l])
            xc = xc + cg1 * (jnp.concatenate([merge_heads(attn_c), pool_c], axis=-1) @ w_out[l])

        h2 = rmsnorm(x, norm2[l]) * (1 + sc2) + sh2
        if not last:
            h2c = rmsnorm(xc, norm2[l]) * (1 + csc2) + csh2
            tokens = jnp.concatenate([h2.reshape(B * L, D), h2c.reshape(B * Lc, D)], axis=0)
            y = grouped_moe(tokens, w_router, router_bias, w_gate[l], w_up[l], w_down[l])
            x = x + g2 * y[:B * L].reshape(B, L, D)
            xc = xc + cg2 * y[B * L:].reshape(B, Lc, D)
        else:
            y = grouped_moe(h2.reshape(B * L, D), w_router, router_bias, w_gate[l], w_up[l], w_down[l])
            x = x + g2 * y.reshape(B, L, D)
    return rmsnorm(x, norm_final)
```

```python
import functools

import numpy as np
import jax
import jax.numpy as jnp
from jax import lax
from jax.experimental import pallas as pl
from jax.experimental.pallas import tpu as pltpu

N_HEADS = 8
HEAD_DIM = 128
GRID_W = 64
WIN_ROWS = 8
WIN_COLS = 16
POOL_WINDOWS = (2, 4, 8, 16)
N_EXPERTS = 32
N_EXPERT_GROUPS = 8
EXPERTS_PER_GROUP = 4
TOP_K = 2
N_MOD = 6
EPS = 1e-6

V7X_LANES = 128
V7X_SUBLANES = 8
V7X_VMEM_LIMIT_BYTES = 56 * 1024 * 1024

TOKEN_TILE = 512
PROJ_CHUNK = 512
MOD_COL_TILE = 1024
ATTN_Q_ROWS = 4
ATTN_K_ROWS = 12
POOL_TILE = 256
EXPERT_BLOCK = 256
FF_CHUNK = 512
MASK_VALUE = -1e30

F32 = jnp.float32
BF16 = jnp.bfloat16


def _round_up(a, b):
    return (a + b - 1) // b * b


def _compiler_params(semantics):
    return pltpu.CompilerParams(dimension_semantics=semantics,
                                vmem_limit_bytes=V7X_VMEM_LIMIT_BYTES)


def _mod_kernel(a_ref, w_ref, b_ref, o_ref):
    a = a_ref[...]
    a = (a * jax.nn.sigmoid(a)).astype(BF16)
    o_ref[...] = jnp.dot(a, w_ref[...].astype(BF16), preferred_element_type=F32) + b_ref[...]


def _modulation(cc, w_mod, b_mod):
    depth, d, n = w_mod.shape
    rows = cc.shape[0]
    tn = MOD_COL_TILE
    return pl.pallas_call(
        _mod_kernel,
        grid=(depth, n // tn),
        in_specs=[
            pl.BlockSpec((rows, d), lambda l, j: (0, 0)),
            pl.BlockSpec((None, d, tn), lambda l, j: (l, 0, j)),
            pl.BlockSpec((None, 1, tn), lambda l, j: (l, 0, j)),
        ],
        out_specs=pl.BlockSpec((None, rows, tn), lambda l, j: (l, 0, j)),
        out_shape=jax.ShapeDtypeStruct((depth, rows, n), F32),
        compiler_params=_compiler_params(("arbitrary", "arbitrary")),
        name="modulation",
    )(cc, w_mod, b_mod.reshape(depth, 1, n))


def _rms_modulate(x, gain, shift, scale):
    y = x * lax.rsqrt(jnp.mean(x * x, axis=-1, keepdims=True) + EPS)
    return (y * gain) * (1.0 + scale) + shift


def _norm_proj_kernel(*refs, has_y, q_cols, q_scale):
    if has_y:
        x_ref, y_ref, modp_ref, mod_ref, g_ref, w_ref, xo_ref, o_ref = refs
        x = x_ref[...] + modp_ref[5:6, :] * y_ref[...]
        xo_ref[...] = x
    else:
        x_ref, mod_ref, g_ref, w_ref, o_ref = refs
        x = x_ref[...]
    hb = _rms_modulate(x, g_ref[...], mod_ref[0:1, :], mod_ref[1:2, :]).astype(BF16)
    n_out = o_ref.shape[1]
    for n0 in range(0, n_out, PROJ_CHUNK):
        acc = jnp.dot(hb, w_ref[:, n0:n0 + PROJ_CHUNK], preferred_element_type=F32)
        if n0 < q_cols:
            acc = acc * q_scale
        o_ref[:, n0:n0 + PROJ_CHUNK] = acc.astype(o_ref.dtype)


def _norm_proj(x, y, mod_prev, mod, gain, w_bf16, mod_row):
    nt, d = x.shape
    n_out = w_bf16.shape[1]
    tm = TOKEN_TILE
    has_y = y is not None
    row_spec = pl.BlockSpec((tm, d), lambda i: (i, 0))
    mod_spec = pl.BlockSpec((None, V7X_SUBLANES, d), lambda i: (mod_row(i), 0, 0))
    in_specs = [row_spec]
    args = [x]
    if has_y:
        in_specs += [row_spec, mod_spec]
        args += [y, mod_prev]
    in_specs += [
        mod_spec,
        pl.BlockSpec((1, d), lambda i: (0, 0)),
        pl.BlockSpec((d, n_out), lambda i: (0, 0), pipeline_mode=pl.Buffered(1)),
    ]
    args += [mod, gain.reshape(1, d), w_bf16]
    out_specs = [pl.BlockSpec((tm, n_out), lambda i: (i, 0))]
    out_shape = [jax.ShapeDtypeStruct((nt, n_out), BF16)]
    if has_y:
        out_specs = [row_spec] + out_specs
        out_shape = [jax.ShapeDtypeStruct((nt, d), F32)] + out_shape
    kern = functools.partial(_norm_proj_kernel, has_y=has_y, q_cols=N_HEADS * HEAD_DIM,
                             q_scale=HEAD_DIM ** -0.5)
    outs = pl.pallas_call(
        kern, grid=(nt // tm,), in_specs=in_specs, out_specs=out_specs, out_shape=out_shape,
        compiler_params=_compiler_params(("arbitrary",)),
        name="norm_in_proj",
    )(*args)
    if has_y:
        return outs[0], outs[1]
    return None, outs[0]


def _attn_kernel(q_ref, k_ref, v_ref, kc_ref, vc_ref, tab_ref, o_ref, *, n_rows):
    gq = ATTN_Q_ROWS * GRID_W
    wk = ATTN_K_ROWS * GRID_W
    n_groups = n_rows // ATTN_Q_ROWS
    kc = kc_ref[...]
    vc = vc_ref[...]
    nt_dims = (((1,), (1,)), ((), ()))

    def body(gi, carry):
        q0 = pl.multiple_of(gi * gq, gq)
        ws = jnp.clip(gi * ATTN_Q_ROWS - WIN_ROWS // 2, 0, n_rows - ATTN_K_ROWS)
        k0 = pl.multiple_of(ws * GRID_W, GRID_W)
        q = q_ref[pl.ds(q0, gq), :]
        kw = k_ref[pl.ds(k0, wk), :]
        vw = v_ref[pl.ds(k0, wk), :]
        kind = jnp.where(gi == 0, 0, jnp.where(gi == n_groups - 1, 2, 1))
        s_loc = lax.dot_general(q, kw, nt_dims, preferred_element_type=F32) + tab_ref[kind]
        s_ctx = lax.dot_general(q, kc, nt_dims, preferred_element_type=F32)
        m = jnp.maximum(jnp.max(s_loc, axis=1, keepdims=True), jnp.max(s_ctx, axis=1, keepdims=True))
        p_loc = jnp.exp(s_loc - m)
        p_ctx = jnp.exp(s_ctx - m)
        den = jnp.sum(p_loc, axis=1, keepdims=True) + jnp.sum(p_ctx, axis=1, keepdims=True)
        o = (jnp.dot(p_loc.astype(BF16), vw, preferred_element_type=F32)
             + jnp.dot(p_ctx.astype(BF16), vc, preferred_element_type=F32))
        o_ref[pl.ds(q0, gq), :] = (o / den).astype(o_ref.dtype)
        return carry

    lax.fori_loop(0, n_groups, body, 0)


def _attn_bias_tables(rpb_l, n_rows):
    n_groups = n_rows // ATTN_Q_ROWS

    def pattern(gi):
        ws = int(np.clip(gi * ATTN_Q_ROWS - WIN_ROWS // 2, 0, n_rows - ATTN_K_ROWS))
        r = gi * ATTN_Q_ROWS + np.arange(ATTN_Q_ROWS)[:, None, None, None]
        qc = np.arange(GRID_W)[None, :, None, None]
        kr = ws + np.arange(ATTN_K_ROWS)[None, None, :, None]
        kcol = np.arange(GRID_W)[None, None, None, :]
        kr_n = min(WIN_ROWS, n_rows)
        r0 = np.clip(r - kr_n // 2, 0, n_rows - kr_n)
        c0 = np.clip(qc - WIN_COLS // 2, 0, GRID_W - WIN_COLS)
        valid = (kr >= r0) & (kr < r0 + kr_n) & (kcol >= c0) & (kcol < c0 + WIN_COLS)
        dr = np.clip(kr - r + WIN_ROWS - 1, 0, 2 * WIN_ROWS - 2) + 0 * kcol + 0 * qc
        dc = np.clip(kcol - qc + WIN_COLS - 1, 0, 2 * WIN_COLS - 2) + 0 * kr + 0 * r
        shape = (ATTN_Q_ROWS * GRID_W, ATTN_K_ROWS * GRID_W)
        return (np.broadcast_to(valid, dr.shape).reshape(shape), dr.reshape(shape), dc.reshape(shape))

    pats = [pattern(g) for g in range(n_groups)]
    for g in range(2, n_groups - 1):
        assert all(np.array_equal(a, b) for a, b in zip(pats[1], pats[g]))
    tabs = []
    for g in (0, 1, n_groups - 1):
        valid, dr, dc = pats[g]
        tabs.append(jnp.where(valid[None], rpb_l[:, dr, dc].astype(F32), MASK_VALUE))
    return jnp.stack(tabs, axis=1)


def _attention(qkvu, tables, n_batch, seq, ctx_len, n_rows_out):
    attn_w = N_HEADS * HEAD_DIM
    ctx_blk0 = (n_batch * seq) // ctx_len
    kern = functools.partial(_attn_kernel, n_rows=seq // GRID_W)
    tab_shape = tables.shape[1:]
    return pl.pallas_call(
        kern,
        grid=(n_batch, N_HEADS),
        in_specs=[
            pl.BlockSpec((seq, HEAD_DIM), lambda b, h: (b, h)),
            pl.BlockSpec((seq, HEAD_DIM), lambda b, h: (b, N_HEADS + h)),
            pl.BlockSpec((seq, HEAD_DIM), lambda b, h: (b, 2 * N_HEADS + h)),
            pl.BlockSpec((ctx_len, HEAD_DIM), lambda b, h: (ctx_blk0 + b, N_HEADS + h)),
            pl.BlockSpec((ctx_len, HEAD_DIM), lambda b, h: (ctx_blk0 + b, 2 * N_HEADS + h)),
            pl.BlockSpec((None,) + tab_shape, lambda b, h: (h, 0, 0, 0)),
        ],
        out_specs=pl.BlockSpec((seq, HEAD_DIM), lambda b, h: (b, h)),
        out_shape=jax.ShapeDtypeStruct((n_rows_out, attn_w), BF16),
        compiler_params=_compiler_params(("arbitrary", "arbitrary")),
        name="neighbourhood_attention",
    )(qkvu, qkvu, qkvu, qkvu, qkvu, tables)


def _ctx_attn_kernel(buf_ref, q_ref, k_ref, v_ref, o_ref):
    del buf_ref
    s = lax.dot_general(q_ref[...], k_ref[...], (((1,), (1,)), ((), ())), preferred_element_type=F32)
    m = jnp.max(s, axis=1, keepdims=True)
    p = jnp.exp(s - m)
    den = jnp.sum(p, axis=1, keepdims=True)
    o = jnp.dot(p.astype(BF16), v_ref[...], preferred_element_type=F32)
    o_ref[...] = (o / den).astype(o_ref.dtype)


def _context_attention(attn, qkvu, n_batch, seq, ctx_len):
    blk0 = (n_batch * seq) // ctx_len
    return pl.pallas_call(
        _ctx_attn_kernel,
        grid=(n_batch, N_HEADS),
        in_specs=[
            pl.BlockSpec(memory_space=pl.ANY),
            pl.BlockSpec((ctx_len, HEAD_DIM), lambda b, h: (blk0 + b, h)),
            pl.BlockSpec((ctx_len, HEAD_DIM), lambda b, h: (blk0 + b, N_HEADS + h)),
            pl.BlockSpec((ctx_len, HEAD_DIM), lambda b, h: (blk0 + b, 2 * N_HEADS + h)),
        ],
        out_specs=pl.BlockSpec((ctx_len, HEAD_DIM), lambda b, h: (blk0 + b, h)),
        out_shape=jax.ShapeDtypeStruct(attn.shape, attn.dtype),
        input_output_aliases={0: 0},
        compiler_params=_compiler_params(("arbitrary", "arbitrary")),
        name="context_attention",
    )(attn, qkvu, qkvu, qkvu)


def _pool_kernel(*refs, seq, win, aliased):
    if aliased:
        _, u_ref, band_ref, w_ref, sc_ref, o_ref = refs
    else:
        u_ref, band_ref, w_ref, sc_ref, o_ref = refs
    g = pl.program_id(1)
    j = pl.program_id(2)
    t = POOL_TILE
    ws = pl.multiple_of(jnp.clip(j * t - (win - t) // 2, 0, seq - win), V7X_LANES)
    t0 = pl.multiple_of(j * t, t)
    u_win = u_ref[pl.ds(ws, win), :]
    u_tok = u_ref[pl.ds(t0, t), :].astype(F32)
    total = jnp.dot(band_ref[...], u_win, preferred_element_type=F32)
    left = jnp.left_shift(1, g)
    pos = t0 + lax.broadcasted_iota(jnp.int32, total.shape, 0)
    cnt = jnp.minimum(pos + left, seq) - jnp.maximum(pos - left, 0)
    pooled = (total / cnt.astype(F32) - u_tok).astype(BF16)
    y = jnp.dot(pooled, w_ref[...], preferred_element_type=F32)
    o_ref[...] = (y * sc_ref[...]).astype(o_ref.dtype)


def _pool_bands(seq):
    t = POOL_TILE
    win = min(2 * t, seq)
    n_tiles = seq // t
    starts = [int(np.clip(j * t - (win - t) // 2, 0, seq - win)) for j in range(n_tiles)]

    def band(w, j):
        left, right = w // 2, w - 1 - w // 2
        pos = j * t + np.arange(t)[:, None]
        src = starts[j] + np.arange(win)[None, :]
        return ((src >= np.maximum(pos - left, 0)) & (src < np.minimum(pos + right + 1, seq))).astype(np.float32)

    kinds = sorted({0, min(1, n_tiles - 1), n_tiles - 1})
    for w in POOL_WINDOWS:
        for j in range(2, n_tiles - 1):
            assert np.array_equal(band(w, 1), band(w, j))
    bands = np.stack([np.stack([band(w, j) for j in kinds]) for w in POOL_WINDOWS])
    return jnp.asarray(bands, BF16), win, n_tiles, len(kinds)


def _pool(prev, qkvu, w_pool_bf16, pool_scale, n_seqs, seq, row0, n_rows_out):
    n_g = len(POOL_WINDOWS)
    cg = w_pool_bf16.shape[-1]
    u_col0 = (3 * N_HEADS * HEAD_DIM) // cg
    bands, win, n_tiles, n_kinds = _pool_bands(seq)
    t = POOL_TILE
    seq_blk0 = row0 // seq
    tile_blk0 = row0 // t

    def kind(j):
        return jnp.where(j == 0, 0, jnp.where(j == n_tiles - 1, n_kinds - 1, 1))

    aliased = prev is not None
    in_specs = [
        pl.BlockSpec((seq, cg), lambda b, g, j: (seq_blk0 + b, u_col0 + g)),
        pl.BlockSpec((None, None, t, win), lambda b, g, j: (g, kind(j), 0, 0)),
        pl.BlockSpec((None, cg, cg), lambda b, g, j: (g, 0, 0)),
        pl.BlockSpec((1, cg), lambda b, g, j: (0, g)),
    ]
    args = [qkvu, bands, w_pool_bf16, pool_scale.reshape(1, n_g * cg)]
    if aliased:
        in_specs = [pl.BlockSpec(memory_space=pl.ANY)] + in_specs
        args = [prev] + args
    return pl.pallas_call(
        functools.partial(_pool_kernel, seq=seq, win=win, aliased=aliased),
        grid=(n_seqs, n_g, n_tiles),
        in_specs=in_specs,
        out_specs=pl.BlockSpec((t, cg), lambda b, g, j: (tile_blk0 + b * n_tiles + j, g)),
        out_shape=jax.ShapeDtypeStruct((n_rows_out, n_g * cg), BF16),
        input_output_aliases={0: 0} if aliased else {},
        compiler_params=_compiler_params(("arbitrary", "arbitrary", "arbitrary")),
        name="multiscale_pool",
    )(*args)


def _first_max_index(vals):
    best = vals[0]
    for v in vals[1:]:
        best = jnp.maximum(best, v)
    idx = jnp.full(best.shape, len(vals), jnp.int32)
    for j in reversed(range(len(vals))):
        idx = jnp.where(vals[j] == best, j, idx)
    return best, idx


def _out_router_kernel(a_ref, p_ref, x_ref, mod_ref, g_ref, wo_ref, wr_ref, rb_ref,
                       xo_ref, h_ref, r_ref):
    aw = a_ref.shape[1]
    mix = (jnp.dot(a_ref[...], wo_ref[0:aw, :], preferred_element_type=F32)
           + jnp.dot(p_ref[...], wo_ref[aw:, :], preferred_element_type=F32))
    x = x_ref[...] + mod_ref[2:3, :] * mix
    xo_ref[...] = x
    hb = _rms_modulate(x, g_ref[...], mod_ref[3:4, :], mod_ref[4:5, :]).astype(BF16)
    h_ref[...] = hb

    logits = jnp.dot(hb, wr_ref[...], preferred_element_type=F32)
    s = jax.nn.sigmoid(logits)
    sb = s + rb_ref[...]
    lane = lax.broadcasted_iota(jnp.int32, sb.shape, 1)
    n_g = N_EXPERT_GROUPS

    def member(arr, j):
        return arr if j == 0 else pltpu.roll(arr, V7X_LANES - n_g * j, axis=1)

    sb_j = [member(sb, j) for j in range(EXPERTS_PER_GROUP)]
    s_j = [member(s, j) for j in range(EXPERTS_PER_GROUP)]
    hi1, lo1 = jnp.maximum(sb_j[0], sb_j[1]), jnp.minimum(sb_j[0], sb_j[1])
    hi2, lo2 = jnp.maximum(sb_j[2], sb_j[3]), jnp.minimum(sb_j[2], sb_j[3])
    top1 = jnp.maximum(hi1, hi2)
    top2 = jnp.maximum(jnp.minimum(hi1, hi2), jnp.maximum(lo1, lo2))
    grp = jnp.where(lane < n_g, top1 + top2, -jnp.inf)
    gmax = jnp.max(grp, axis=1, keepdims=True)
    gidx = jnp.min(jnp.where(grp == gmax, lane, V7X_LANES), axis=1, keepdims=True)
    sel = lane == gidx
    sbv = [jnp.sum(jnp.where(sel, v, 0.0), axis=1, keepdims=True) for v in sb_j]
    sv = [jnp.sum(jnp.where(sel, v, 0.0), axis=1, keepdims=True) for v in s_j]
    _, loc0 = _first_max_index(sbv)
    _, loc1 = _first_max_index([jnp.where(loc0 == j, -jnp.inf, sbv[j]) for j in range(EXPERTS_PER_GROUP)])
    pick = lambda loc: sum(jnp.where(loc == j, sv[j], 0.0) for j in range(EXPERTS_PER_GROUP))
    s0, s1 = pick(loc0), pick(loc1)
    tot = s0 + s1
    e0 = (gidx * EXPERTS_PER_GROUP + loc0).astype(F32)
    e1 = (gidx * EXPERTS_PER_GROUP + loc1).astype(F32)
    r_ref[...] = jnp.where(lane == 0, e0, jnp.where(lane == 1, e1,
                           jnp.where(lane == 2, s0 / tot, jnp.where(lane == 3, s1 / tot, 0.0))))


def _out_router(attn, pool, x, mod, gain, w_out_bf16, w_router_perm, rb_perm, n_rows, mod_row):
    nt, d = x.shape
    aw = attn.shape[1]
    pw = pool.shape[1]
    tm = TOKEN_TILE
    row = lambda w: pl.BlockSpec((tm, w), lambda i: (i, 0))
    return pl.pallas_call(
        _out_router_kernel,
        grid=(n_rows // tm,),
        in_specs=[
            row(aw), row(pw), row(d),
            pl.BlockSpec((None, V7X_SUBLANES, d), lambda i: (mod_row(i), 0, 0)),
            pl.BlockSpec((1, d), lambda i: (0, 0)),
            pl.BlockSpec((aw + pw, d), lambda i: (0, 0), pipeline_mode=pl.Buffered(1)),
            pl.BlockSpec((d, V7X_LANES), lambda i: (0, 0)),
            pl.BlockSpec((1, V7X_LANES), lambda i: (0, 0)),
        ],
        out_specs=[row(d), row(d), row(V7X_LANES)],
        out_shape=[jax.ShapeDtypeStruct((n_rows, d), F32),
                   jax.ShapeDtypeStruct((n_rows, d), BF16),
                   jax.ShapeDtypeStruct((n_rows, V7X_LANES), F32)],
        compiler_params=_compiler_params(("arbitrary",)),
        name="out_proj_router",
    )(attn, pool, x, mod, gain.reshape(1, d), w_out_bf16, w_router_perm, rb_perm)


def _moe_kernel(blk_e_ref, n_used_ref, x_ref, wg_ref, wu_ref, wd_ref, o_ref):
    del blk_e_ref
    i = pl.program_id(0)

    @pl.when(i < n_used_ref[0])
    def _():
        x = x_ref[...]
        ff = wg_ref.shape[1]
        acc = None
        for f0 in range(0, ff, FF_CHUNK):
            g = jnp.dot(x, wg_ref[:, f0:f0 + FF_CHUNK], preferred_element_type=F32)
            u = jnp.dot(x, wu_ref[:, f0:f0 + FF_CHUNK], preferred_element_type=F32)
            a = ((g * jax.nn.sigmoid(g)) * u).astype(BF16)
            part = jnp.dot(a, wd_ref[f0:f0 + FF_CHUNK, :], preferred_element_type=F32)
            acc = part if acc is None else acc + part
        o_ref[...] = acc.astype(o_ref.dtype)

    @pl.when(i >= n_used_ref[0])
    def _():
        o_ref[...] = jnp.zeros(o_ref.shape, o_ref.dtype)


def _expert_mlp(xb, blk_e, n_used, wg, wu, wd):
    p, d = xb.shape
    ff = wg.shape[-1]
    bm = EXPERT_BLOCK
    grid_spec = pltpu.PrefetchScalarGridSpec(
        num_scalar_prefetch=2,
        grid=(p // bm,),
        in_specs=[
            pl.BlockSpec((bm, d), lambda i, be, nu: (i, 0)),
            pl.BlockSpec((None, d, ff), lambda i, be, nu: (be[i], 0, 0)),
            pl.BlockSpec((None, d, ff), lambda i, be, nu: (be[i], 0, 0)),
            pl.BlockSpec((None, ff, d), lambda i, be, nu: (be[i], 0, 0)),
        ],
        out_specs=pl.BlockSpec((bm, d), lambda i, be, nu: (i, 0)),
    )
    return pl.pallas_call(
        _moe_kernel, grid_spec=grid_spec,
        out_shape=jax.ShapeDtypeStruct((p, d), BF16),
        compiler_params=_compiler_params(("arbitrary",)),
        name="expert_mlp",
    )(blk_e, n_used, xb, wg, wu, wd)


def _dispatch_plan(expert, n_tokens):
    bm = EXPERT_BLOCK
    a = n_tokens * TOP_K
    n_blk = (a + N_EXPERTS * (bm - 1)) // bm + 1
    flat_e = expert.reshape(-1)
    onehot = (flat_e[:, None] == jnp.arange(N_EXPERTS, dtype=jnp.int32)[None, :]).astype(jnp.int32)
    csum = jnp.cumsum(onehot, axis=0)
    rank = jnp.take_along_axis(csum, flat_e[:, None], axis=1)[:, 0] - 1
    counts = csum[-1]
    padded = (counts + bm - 1) // bm * bm
    pend = jnp.cumsum(padded)
    dest = (pend - padded)[flat_e] + rank
    row_src = jnp.zeros((n_blk * bm,), jnp.int32).at[dest].set(jnp.arange(a, dtype=jnp.int32) // TOP_K)
    blk_start = jnp.arange(n_blk, dtype=jnp.int32) * bm
    blk_e = jnp.minimum(jnp.sum(blk_start[:, None] >= pend[None, :], axis=1), N_EXPERTS - 1).astype(jnp.int32)
    n_used = (pend[-1] // bm).astype(jnp.int32).reshape(1)
    return dest.reshape(n_tokens, TOP_K), row_src, blk_e, n_used


def _moe(h2, route, wg, wu, wd):
    n_tokens = h2.shape[0]
    expert = route[:, :TOP_K].astype(jnp.int32)
    gates = route[:, TOP_K:2 * TOP_K]
    dest, row_src, blk_e, n_used = _dispatch_plan(expert, n_tokens)
    xb = jnp.take(h2, row_src, axis=0)
    yb = _expert_mlp(xb, blk_e, n_used, wg, wu, wd)
    y = (gates[:, 0:1] * jnp.take(yb, dest[:, 0], axis=0).astype(F32)
         + gates[:, 1:2] * jnp.take(yb, dest[:, 1], axis=0).astype(F32))
    return y


def _final_kernel(x_ref, y_ref, mod_ref, g_ref, o_ref):
    x = x_ref[...] + mod_ref[5:6, :] * y_ref[...]
    o_ref[...] = (x * lax.rsqrt(jnp.mean(x * x, axis=-1, keepdims=True) + EPS)) * g_ref[...]


def _final(x, y, mod, gain, n_rows, mod_row):
    d = x.shape[1]
    tm = TOKEN_TILE
    row_spec = pl.BlockSpec((tm, d), lambda i: (i, 0))
    return pl.pallas_call(
        _final_kernel,
        grid=(n_rows // tm,),
        in_specs=[row_spec, row_spec,
                  pl.BlockSpec((None, V7X_SUBLANES, d), lambda i: (mod_row(i), 0, 0)),
                  pl.BlockSpec((1, d), lambda i: (0, 0))],
        out_specs=row_spec,
        out_shape=jax.ShapeDtypeStruct((n_rows, d), F32),
        compiler_params=_compiler_params(("arbitrary",)),
        name="final_norm",
    )(x, y, mod, gain.reshape(1, d))


def kernel(x, c, ctx, c_ctx, w_mod, b_mod, norm1, norm2, w_in, rpb, w_pool, pool_scale, w_out,
           w_router, router_bias, w_gate, w_up, w_down, norm_final):
    n_batch, seq, d = x.shape
    ctx_len = ctx.shape[1]
    depth = w_mod.shape[0]
    n_lat = n_batch * seq
    n_ctx = n_batch * ctx_len
    nt = n_lat + n_ctx
    tm = TOKEN_TILE
    assert seq % tm == 0 and n_ctx % tm == 0 and seq % (GRID_W * ATTN_Q_ROWS) == 0
    assert seq % ctx_len == 0 and ctx_len % POOL_TILE == 0 and d == 2 * N_HEADS * HEAD_DIM

    mod_rows = _round_up(n_batch + 1, V7X_SUBLANES)
    cc = jnp.zeros((mod_rows, d), F32).at[:n_batch].set(c).at[n_batch].set(c_ctx)
    mod_all = _modulation(cc, w_mod, b_mod).reshape(depth, mod_rows, N_MOD, d)
    mod_all = jnp.pad(mod_all, ((0, 0), (0, 0), (0, V7X_SUBLANES - N_MOD), (0, 0)))
    n_lat_tiles = n_lat // tm
    mod_row = lambda i: jnp.where(i < n_lat_tiles, i // (seq // tm), n_batch)

    lanes = np.arange(N_EXPERTS)
    perm = (lanes % N_EXPERT_GROUPS) * EXPERTS_PER_GROUP + lanes // N_EXPERT_GROUPS
    w_router_perm = jnp.zeros((d, V7X_LANES), BF16).at[:, :N_EXPERTS].set(w_router[:, perm].astype(BF16))
    rb_perm = jnp.zeros((1, V7X_LANES), F32).at[0, :N_EXPERTS].set(router_bias[perm].astype(F32))

    xs = jnp.concatenate([x.reshape(n_lat, d), ctx.reshape(n_ctx, d)], axis=0)
    y = None
    for l in range(depth):
        last = l == depth - 1
        n_rows = n_lat if last else nt
        w_in_l = w_in[l].astype(BF16)
        xs_new, qkvu = _norm_proj(xs, y, mod_all[l - 1] if l else None, mod_all[l], norm1[l],
                                  w_in_l, mod_row)
        if xs_new is not None:
            xs = xs_new
        tables = _attn_bias_tables(rpb[l], seq // GRID_W)
        attn = _attention(qkvu, tables, n_batch, seq, ctx_len, n_rows)
        w_pool_l = w_pool[l].astype(BF16)
        pool = _pool(None, qkvu, w_pool_l, pool_scale[l], n_batch, seq, 0, n_rows)
        if not last:
            attn = _context_attention(attn, qkvu, n_batch, seq, ctx_len)
            pool = _pool(pool, qkvu, w_pool_l, pool_scale[l], n_batch, ctx_len, n_lat, n_rows)
        xs_mid, h2, route = _out_router(attn, pool, xs, mod_all[l], norm2[l], w_out[l].astype(BF16),
                                        w_router_perm, rb_perm, n_rows, mod_row)
        if last:
            xs = xs_mid
        else:
            xs = xs_mid
        y = _moe(h2, route, w_gate[l].astype(BF16), w_up[l].astype(BF16), w_down[l].astype(BF16))
    out = _final(xs, y, mod_all[depth - 1], norm_final, n_lat, mod_row)
    return out.reshape(n_batch, seq, d)
```

```python
import functools

import numpy as np
import jax
import jax.numpy as jnp
from jax import lax
from jax.experimental import pallas as pl
from jax.experimental.pallas import tpu as pltpu

N_HEADS = 8
HEAD_DIM = 128
GRID_W = 64
WIN_ROWS = 8
WIN_COLS = 16
POOL_WINDOWS = (2, 4, 8, 16)
N_EXPERTS = 32
N_EXPERT_GROUPS = 8
EXPERTS_PER_GROUP = 4
TOP_K = 2
N_MOD = 6
EPS = 1e-6

V7X_LANES = 128
V7X_SUBLANES = 8
V7X_VMEM_LIMIT_BYTES = 56 * 1024 * 1024

TOKEN_TILE = 512
PROJ_CHUNK = 512
MOD_COL_TILE = 1024
ATTN_Q_ROWS = 4
ATTN_K_ROWS = 12
POOL_TILE = 256
EXPERT_BLOCK = 256
FF_CHUNK = 512
MASK_VALUE = -1e30

F32 = jnp.float32
BF16 = jnp.bfloat16


def _round_up(a, b):
    return (a + b - 1) // b * b


def _compiler_params(semantics):
    return pltpu.CompilerParams(dimension_semantics=semantics,
                                vmem_limit_bytes=V7X_VMEM_LIMIT_BYTES)


def _mod_kernel(a_ref, w_ref, b_ref, o_ref):
    a = a_ref[...]
    a = (a * jax.nn.sigmoid(a)).astype(BF16)
    o_ref[...] = jnp.dot(a, w_ref[...].astype(BF16), preferred_element_type=F32) + b_ref[...]


def _modulation(cc, w_mod, b_mod):
    depth, d, n = w_mod.shape
    rows = cc.shape[0]
    tn = MOD_COL_TILE
    return pl.pallas_call(
        _mod_kernel,
        grid=(depth, n // tn),
        in_specs=[
            pl.BlockSpec((rows, d), lambda l, j: (0, 0)),
            pl.BlockSpec((None, d, tn), lambda l, j: (l, 0, j)),
            pl.BlockSpec((None, 1, tn), lambda l, j: (l, 0, j)),
        ],
        out_specs=pl.BlockSpec((None, rows, tn), lambda l, j: (l, 0, j)),
        out_shape=jax.ShapeDtypeStruct((depth, rows, n), F32),
        compiler_params=_compiler_params(("arbitrary", "arbitrary")),
        name="modulation",
    )(cc, w_mod, b_mod.reshape(depth, 1, n))


def _rms_modulate(x, gain, shift, scale):
    y = x * lax.rsqrt(jnp.mean(x * x, axis=-1, keepdims=True) + EPS)
    return (y * gain) * (1.0 + scale) + shift


def _project(hb, w_ref, o_ref, q_cols, q_scale):
    for n0 in range(0, o_ref.shape[1], PROJ_CHUNK):
        acc = jnp.dot(hb, w_ref[:, n0:n0 + PROJ_CHUNK], preferred_element_type=F32)
        if n0 < q_cols:
            acc = acc * q_scale
        o_ref[:, n0:n0 + PROJ_CHUNK] = acc.astype(o_ref.dtype)


def _first_proj_kernel(xl_ref, xc_ref, mod_ref, g_ref, w_ref, o_ref, *, n_lat_tiles, q_cols, q_scale):
    x = jnp.where(pl.program_id(0) < n_lat_tiles, xl_ref[...], xc_ref[...])
    hb = _rms_modulate(x, g_ref[...], mod_ref[0:1, :], mod_ref[1:2, :]).astype(BF16)
    _project(hb, w_ref, o_ref, q_cols, q_scale)


def _next_proj_kernel(x_ref, y0_ref, y1_ref, gate_ref, modp_ref, mod_ref, g_ref, w_ref, xo_ref, o_ref, *,
                      q_cols, q_scale):
    x = x_ref[...] + modp_ref[5:6, :] * _combine(y0_ref, y1_ref, gate_ref)
    xo_ref[...] = x
    hb = _rms_modulate(x, g_ref[...], mod_ref[0:1, :], mod_ref[1:2, :]).astype(BF16)
    _project(hb, w_ref, o_ref, q_cols, q_scale)


def _combine(y0_ref, y1_ref, gate_ref):
    g = gate_ref[...]
    return g[:, 2:3] * y0_ref[...].astype(F32) + g[:, 3:4] * y1_ref[...].astype(F32)


def _split_rows_specs(tm, d, n_lat_tiles):
    return (pl.BlockSpec((tm, d), lambda i: (jnp.minimum(i, n_lat_tiles - 1), 0)),
            pl.BlockSpec((tm, d), lambda i: (jnp.maximum(i - n_lat_tiles, 0), 0)))


def _first_proj(x_lat, x_ctx, mod_all, layer, gain, w_bf16, mod_row):
    n_lat, d = x_lat.shape
    nt = n_lat + x_ctx.shape[0]
    n_out = w_bf16.shape[-1]
    tm = TOKEN_TILE
    n_lat_tiles = n_lat // tm
    kern = functools.partial(_first_proj_kernel, n_lat_tiles=n_lat_tiles, q_cols=N_HEADS * HEAD_DIM,
                             q_scale=HEAD_DIM ** -0.5)
    return pl.pallas_call(
        kern, grid=(nt // tm,),
        in_specs=[
            *_split_rows_specs(tm, d, n_lat_tiles),
            pl.BlockSpec((None, None, V7X_SUBLANES, d), lambda i: (layer, mod_row(i), 0, 0)),
            pl.BlockSpec((1, d), lambda i: (0, 0)),
            pl.BlockSpec((None, d, n_out), lambda i: (layer, 0, 0), pipeline_mode=pl.Buffered(1)),
        ],
        out_specs=pl.BlockSpec((tm, n_out), lambda i: (i, 0)),
        out_shape=jax.ShapeDtypeStruct((nt, n_out), BF16),
        compiler_params=_compiler_params(("arbitrary",)),
        name="norm_in_proj_first",
    )(x_lat, x_ctx, mod_all, gain.reshape(1, d), w_bf16)


def _next_proj(x, y2, route, mod_all, layer, gain, w_bf16, mod_row):
    nt, d = x.shape
    n_out = w_bf16.shape[-1]
    tm = TOKEN_TILE
    row_spec = pl.BlockSpec((tm, d), lambda i: (i, 0))
    mod_spec = lambda l: pl.BlockSpec((None, None, V7X_SUBLANES, d), lambda i: (l, mod_row(i), 0, 0))
    kern = functools.partial(_next_proj_kernel, q_cols=N_HEADS * HEAD_DIM, q_scale=HEAD_DIM ** -0.5)
    return pl.pallas_call(
        kern, grid=(nt // tm,),
        in_specs=[
            row_spec, row_spec, row_spec,
            pl.BlockSpec((tm, V7X_LANES), lambda i: (i, 0)),
            mod_spec(layer - 1), mod_spec(layer),
            pl.BlockSpec((1, d), lambda i: (0, 0)),
            pl.BlockSpec((None, d, n_out), lambda i: (layer, 0, 0), pipeline_mode=pl.Buffered(1)),
        ],
        out_specs=[row_spec, pl.BlockSpec((tm, n_out), lambda i: (i, 0))],
        out_shape=[jax.ShapeDtypeStruct((nt, d), F32), jax.ShapeDtypeStruct((nt, n_out), BF16)],
        compiler_params=_compiler_params(("arbitrary",)),
        name="norm_in_proj_next",
    )(x, *y2, route, mod_all, mod_all, gain.reshape(1, d), w_bf16)


def _attn_window_start(gi, n_rows):
    return int(np.clip(gi * ATTN_Q_ROWS - WIN_ROWS // 2, 0, n_rows - ATTN_K_ROWS))


def _attn_kernel(q_ref, k_ref, v_ref, kc_ref, vc_ref, tab_ref, o_ref, *, n_rows):
    gq = ATTN_Q_ROWS * GRID_W
    wk = ATTN_K_ROWS * GRID_W
    n_groups = n_rows // ATTN_Q_ROWS
    kc = kc_ref[...]
    vc = vc_ref[...]
    nt_dims = (((1,), (1,)), ((), ()))
    for gi in range(n_groups):
        q0 = gi * gq
        k0 = _attn_window_start(gi, n_rows) * GRID_W
        kind = 0 if gi == 0 else (2 if gi == n_groups - 1 else 1)
        q = q_ref[q0:q0 + gq, :]
        kw = k_ref[k0:k0 + wk, :]
        vw = v_ref[k0:k0 + wk, :]
        s_loc = lax.dot_general(q, kw, nt_dims, preferred_element_type=F32) + tab_ref[kind]
        s_ctx = lax.dot_general(q, kc, nt_dims, preferred_element_type=F32)
        m = jnp.maximum(jnp.max(s_loc, axis=1, keepdims=True), jnp.max(s_ctx, axis=1, keepdims=True))
        p_loc = jnp.exp(s_loc - m)
        p_ctx = jnp.exp(s_ctx - m)
        den = jnp.sum(p_loc, axis=1, keepdims=True) + jnp.sum(p_ctx, axis=1, keepdims=True)
        o = (jnp.dot(p_loc.astype(BF16), vw, preferred_element_type=F32)
             + jnp.dot(p_ctx.astype(BF16), vc, preferred_element_type=F32))
        o_ref[q0:q0 + gq, :] = (o / den).astype(o_ref.dtype)


def _attn_bias_tables(rpb, n_rows):
    n_groups = n_rows // ATTN_Q_ROWS
    kr_n = min(WIN_ROWS, n_rows)

    def row_pattern(gi):
        ws = _attn_window_start(gi, n_rows)
        r = gi * ATTN_Q_ROWS + np.arange(ATTN_Q_ROWS)[:, None]
        kr = ws + np.arange(ATTN_K_ROWS)[None, :]
        r0 = np.clip(r - kr_n // 2, 0, n_rows - kr_n)
        valid = (kr >= r0) & (kr < r0 + kr_n)
        return valid, np.clip(kr - r + WIN_ROWS - 1, 0, 2 * WIN_ROWS - 2)

    pats = [row_pattern(g) for g in range(n_groups)]
    for g in range(2, n_groups - 1):
        assert all(np.array_equal(a, b) for a, b in zip(pats[1], pats[g]))
    kinds = (0, 1, n_groups - 1)
    valid_r = np.stack([pats[g][0] for g in kinds])
    dr = np.stack([pats[g][1] for g in kinds])
    qc = np.arange(GRID_W)[:, None]
    kc = np.arange(GRID_W)[None, :]
    c0 = np.clip(qc - WIN_COLS // 2, 0, GRID_W - WIN_COLS)
    valid_c = (kc >= c0) & (kc < c0 + WIN_COLS)
    dc = np.clip(kc - qc + WIN_COLS - 1, 0, 2 * WIN_COLS - 2)
    rsel = jnp.asarray(np.eye(2 * WIN_ROWS - 1, dtype=np.float32)[dr])
    csel = jnp.asarray(np.eye(2 * WIN_COLS - 1, dtype=np.float32)[dc])
    a = jnp.einsum("tqkr,lhrc->lhtqkc", rsel, rpb.astype(F32), precision=lax.Precision.HIGHEST)
    t = jnp.einsum("lhtqkc,pjc->lhtqpkj", a, csel, precision=lax.Precision.HIGHEST)
    valid = valid_r[:, :, None, :, None] & valid_c[None, None, :, None, :]
    t = jnp.where(jnp.asarray(valid)[None, None], t, MASK_VALUE)
    return t.reshape(t.shape[:3] + (ATTN_Q_ROWS * GRID_W, ATTN_K_ROWS * GRID_W))


def _attention(qkvu, tables, layer, n_batch, seq, ctx_len, n_rows_out):
    attn_w = N_HEADS * HEAD_DIM
    ctx_blk0 = (n_batch * seq) // ctx_len
    kern = functools.partial(_attn_kernel, n_rows=seq // GRID_W)
    tab_shape = tables.shape[2:]
    return pl.pallas_call(
        kern,
        grid=(n_batch, N_HEADS),
        in_specs=[
            pl.BlockSpec((seq, HEAD_DIM), lambda b, h: (b, h)),
            pl.BlockSpec((seq, HEAD_DIM), lambda b, h: (b, N_HEADS + h)),
            pl.BlockSpec((seq, HEAD_DIM), lambda b, h: (b, 2 * N_HEADS + h)),
            pl.BlockSpec((ctx_len, HEAD_DIM), lambda b, h: (ctx_blk0 + b, N_HEADS + h)),
            pl.BlockSpec((ctx_len, HEAD_DIM), lambda b, h: (ctx_blk0 + b, 2 * N_HEADS + h)),
            pl.BlockSpec((None, None) + tab_shape, lambda b, h: (layer, h, 0, 0, 0)),
        ],
        out_specs=pl.BlockSpec((seq, HEAD_DIM), lambda b, h: (b, h)),
        out_shape=jax.ShapeDtypeStruct((n_rows_out, attn_w), BF16),
        compiler_params=_compiler_params(("arbitrary", "arbitrary")),
        name="neighbourhood_attention",
    )(qkvu, qkvu, qkvu, qkvu, qkvu, tables)


def _ctx_attn_kernel(buf_ref, q_ref, k_ref, v_ref, o_ref):
    del buf_ref
    s = lax.dot_general(q_ref[...], k_ref[...], (((1,), (1,)), ((), ())), preferred_element_type=F32)
    m = jnp.max(s, axis=1, keepdims=True)
    p = jnp.exp(s - m)
    den = jnp.sum(p, axis=1, keepdims=True)
    o = jnp.dot(p.astype(BF16), v_ref[...], preferred_element_type=F32)
    o_ref[...] = (o / den).astype(o_ref.dtype)


def _context_attention(attn, qkvu, n_batch, seq, ctx_len):
    blk0 = (n_batch * seq) // ctx_len
    return pl.pallas_call(
        _ctx_attn_kernel,
        grid=(n_batch, N_HEADS),
        in_specs=[
            pl.BlockSpec(memory_space=pl.ANY),
            pl.BlockSpec((ctx_len, HEAD_DIM), lambda b, h: (blk0 + b, h)),
            pl.BlockSpec((ctx_len, HEAD_DIM), lambda b, h: (blk0 + b, N_HEADS + h)),
            pl.BlockSpec((ctx_len, HEAD_DIM), lambda b, h: (blk0 + b, 2 * N_HEADS + h)),
        ],
        out_specs=pl.BlockSpec((ctx_len, HEAD_DIM), lambda b, h: (blk0 + b, h)),
        out_shape=jax.ShapeDtypeStruct(attn.shape, attn.dtype),
        input_output_aliases={0: 0},
        compiler_params=_compiler_params(("arbitrary", "arbitrary")),
        name="context_attention",
    )(attn, qkvu, qkvu, qkvu)


def _pool_plan(seq):
    t = POOL_TILE
    win = min(2 * t, seq)
    n_tiles = seq // t
    starts = [int(np.clip(j * t - (win - t) // 2, 0, seq - win)) for j in range(n_tiles)]

    def band(w, j):
        left, right = w // 2, w - 1 - w // 2
        pos = j * t + np.arange(t)[:, None]
        src = starts[j] + np.arange(win)[None, :]
        return ((src >= np.maximum(pos - left, 0)) & (src < np.minimum(pos + right + 1, seq))).astype(np.float32)

    kind_tiles = sorted({0, min(1, n_tiles - 1), n_tiles - 1})
    kind_of = [0 if j == 0 else (len(kind_tiles) - 1 if j == n_tiles - 1 else 1) for j in range(n_tiles)]
    for w in POOL_WINDOWS:
        for j in range(n_tiles):
            assert np.array_equal(band(w, kind_tiles[kind_of[j]]), band(w, j))
    bands = np.stack([np.stack([band(w, j) for j in kind_tiles]) for w in POOL_WINDOWS])
    return jnp.asarray(bands, BF16), win, list(zip(starts, kind_of))


def _pool_kernel(*refs, seq, win, tiles, aliased):
    u_ref, band_ref, w_ref, sc_ref, o_ref = refs[1:] if aliased else refs
    t = POOL_TILE
    left = jnp.left_shift(1, pl.program_id(1))
    w = w_ref[...]
    sc = sc_ref[...]
    for j, (ws, kind) in enumerate(tiles):
        t0 = j * t
        total = jnp.dot(band_ref[kind], u_ref[ws:ws + win, :], preferred_element_type=F32)
        pos = t0 + lax.broadcasted_iota(jnp.int32, total.shape, 0)
        cnt = jnp.minimum(pos + left, seq) - jnp.maximum(pos - left, 0)
        pooled = (total / cnt.astype(F32) - u_ref[t0:t0 + t, :].astype(F32)).astype(BF16)
        y = jnp.dot(pooled, w, preferred_element_type=F32)
        o_ref[t0:t0 + t, :] = (y * sc).astype(o_ref.dtype)


def _pool(prev, qkvu, w_pool_bf16, layer, pool_scale, n_seqs, seq, row0, n_rows_out):
    n_g = len(POOL_WINDOWS)
    assert POOL_WINDOWS == tuple(2 << g for g in range(n_g))
    cg = w_pool_bf16.shape[-1]
    u_col0 = (3 * N_HEADS * HEAD_DIM) // cg
    bands, win, tiles = _pool_plan(seq)
    seq_blk0 = row0 // seq
    aliased = prev is not None
    in_specs = [
        pl.BlockSpec((seq, cg), lambda b, g: (seq_blk0 + b, u_col0 + g)),
        pl.BlockSpec((None,) + bands.shape[1:], lambda b, g: (g, 0, 0, 0)),
        pl.BlockSpec((None, None, cg, cg), lambda b, g: (layer, g, 0, 0)),
        pl.BlockSpec((1, cg), lambda b, g: (0, g)),
    ]
    args = [qkvu, bands, w_pool_bf16, pool_scale.reshape(1, n_g * cg)]
    if aliased:
        in_specs = [pl.BlockSpec(memory_space=pl.ANY)] + in_specs
        args = [prev] + args
    return pl.pallas_call(
        functools.partial(_pool_kernel, seq=seq, win=win, tiles=tiles, aliased=aliased),
        grid=(n_seqs, n_g),
        in_specs=in_specs,
        out_specs=pl.BlockSpec((seq, cg), lambda b, g: (seq_blk0 + b, g)),
        out_shape=jax.ShapeDtypeStruct((n_rows_out, n_g * cg), BF16),
        input_output_aliases={0: 0} if aliased else {},
        compiler_params=_compiler_params(("arbitrary", "arbitrary")),
        name="multiscale_pool",
    )(*args)


def _first_max_index(vals):
    best = vals[0]
    for v in vals[1:]:
        best = jnp.maximum(best, v)
    idx = jnp.full(best.shape, len(vals), jnp.int32)
    for j in reversed(range(len(vals))):
        idx = jnp.where(vals[j] == best, j, idx)
    return best, idx


def _route(hb, wr_ref, rb_ref):
    logits = jnp.dot(hb, wr_ref[...], preferred_element_type=F32)
    s = jax.nn.sigmoid(logits)
    sb = s + rb_ref[...]
    lane = lax.broadcasted_iota(jnp.int32, sb.shape, 1)
    n_g = N_EXPERT_GROUPS

    def member(arr, j):
        return arr if j == 0 else pltpu.roll(arr, V7X_LANES - n_g * j, axis=1)

    sb_j = [member(sb, j) for j in range(EXPERTS_PER_GROUP)]
    s_j = [member(s, j) for j in range(EXPERTS_PER_GROUP)]
    hi1, lo1 = jnp.maximum(sb_j[0], sb_j[1]), jnp.minimum(sb_j[0], sb_j[1])
    hi2, lo2 = jnp.maximum(sb_j[2], sb_j[3]), jnp.minimum(sb_j[2], sb_j[3])
    top1 = jnp.maximum(hi1, hi2)
    top2 = jnp.maximum(jnp.minimum(hi1, hi2), jnp.maximum(lo1, lo2))
    grp = jnp.where(lane < n_g, top1 + top2, -jnp.inf)
    gmax = jnp.max(grp, axis=1, keepdims=True)
    lane_f = lane.astype(F32)
    gidx = jnp.min(jnp.where(grp == gmax, lane_f, float(V7X_LANES)), axis=1, keepdims=True)
    sel = lane_f == gidx
    sbv = [jnp.sum(jnp.where(sel, v, 0.0), axis=1, keepdims=True) for v in sb_j]
    sv = [jnp.sum(jnp.where(sel, v, 0.0), axis=1, keepdims=True) for v in s_j]
    _, loc0 = _first_max_index(sbv)
    _, loc1 = _first_max_index([jnp.where(loc0 == j, -jnp.inf, sbv[j]) for j in range(EXPERTS_PER_GROUP)])
    pick = lambda loc: sum(jnp.where(loc == j, sv[j], 0.0) for j in range(EXPERTS_PER_GROUP))
    s0, s1 = pick(loc0), pick(loc1)
    tot = s0 + s1
    e0 = gidx * EXPERTS_PER_GROUP + loc0.astype(F32)
    e1 = gidx * EXPERTS_PER_GROUP + loc1.astype(F32)
    return jnp.where(lane == 0, e0, jnp.where(lane == 1, e1,
                     jnp.where(lane == 2, s0 / tot, jnp.where(lane == 3, s1 / tot, 0.0))))


def _out_router_kernel(*refs, n_lat_tiles):
    if n_lat_tiles is None:
        a_ref, p_ref, x_ref, mod_ref, g_ref, wo_ref, wr_ref, rb_ref, xo_ref, h_ref, r_ref = refs
        x = x_ref[...]
    else:
        a_ref, p_ref, xl_ref, xc_ref, mod_ref, g_ref, wo_ref, wr_ref, rb_ref, xo_ref, h_ref, r_ref = refs
        x = jnp.where(pl.program_id(0) < n_lat_tiles, xl_ref[...], xc_ref[...])
    aw = a_ref.shape[1]
    mix = (jnp.dot(a_ref[...], wo_ref[0:aw, :], preferred_element_type=F32)
           + jnp.dot(p_ref[...], wo_ref[aw:, :], preferred_element_type=F32))
    x = x + mod_ref[2:3, :] * mix
    xo_ref[...] = x
    hb = _rms_modulate(x, g_ref[...], mod_ref[3:4, :], mod_ref[4:5, :]).astype(BF16)
    h_ref[...] = hb
    r_ref[...] = _route(hb, wr_ref, rb_ref)


def _out_router(attn, pool, xs, mod_all, layer, gain, w_out_bf16, w_router_perm, rb_perm, n_rows, mod_row):
    split = isinstance(xs, tuple)
    d = (xs[0] if split else xs).shape[1]
    aw = attn.shape[1]
    pw = pool.shape[1]
    tm = TOKEN_TILE
    row = lambda w: pl.BlockSpec((tm, w), lambda i: (i, 0))
    n_lat_tiles = xs[0].shape[0] // tm if split else None
    x_specs = list(_split_rows_specs(tm, d, n_lat_tiles)) if split else [row(d)]
    return pl.pallas_call(
        functools.partial(_out_router_kernel, n_lat_tiles=n_lat_tiles),
        grid=(n_rows // tm,),
        in_specs=[
            row(aw), row(pw), *x_specs,
            pl.BlockSpec((None, None, V7X_SUBLANES, d), lambda i: (layer, mod_row(i), 0, 0)),
            pl.BlockSpec((1, d), lambda i: (0, 0)),
            pl.BlockSpec((None, aw + pw, d), lambda i: (layer, 0, 0), pipeline_mode=pl.Buffered(1)),
            pl.BlockSpec((d, V7X_LANES), lambda i: (0, 0)),
            pl.BlockSpec((1, V7X_LANES), lambda i: (0, 0)),
        ],
        out_specs=[row(d), row(d), row(V7X_LANES)],
        out_shape=[jax.ShapeDtypeStruct((n_rows, d), F32),
                   jax.ShapeDtypeStruct((n_rows, d), BF16),
                   jax.ShapeDtypeStruct((n_rows, V7X_LANES), F32)],
        compiler_params=_compiler_params(("arbitrary",)),
        name="out_proj_router",
    )(attn, pool, *(xs if split else (xs,)), mod_all, gain.reshape(1, d), w_out_bf16, w_router_perm, rb_perm)


def _moe_kernel(blk_e_ref, n_used_ref, x_ref, wg_ref, wu_ref, wd_ref, o_ref):
    del blk_e_ref
    i = pl.program_id(0)

    @pl.when(i < n_used_ref[0])
    def _():
        x = x_ref[...]
        ff = wg_ref.shape[1]
        acc = None
        for f0 in range(0, ff, FF_CHUNK):
            g = jnp.dot(x, wg_ref[:, f0:f0 + FF_CHUNK], preferred_element_type=F32)
            u = jnp.dot(x, wu_ref[:, f0:f0 + FF_CHUNK], preferred_element_type=F32)
            a = ((g * jax.nn.sigmoid(g)) * u).astype(BF16)
            part = jnp.dot(a, wd_ref[f0:f0 + FF_CHUNK, :], preferred_element_type=F32)
            acc = part if acc is None else acc + part
        o_ref[...] = acc.astype(o_ref.dtype)

    @pl.when(i >= n_used_ref[0])
    def _():
        o_ref[...] = jnp.zeros(o_ref.shape, o_ref.dtype)


def _expert_mlp(xb, blk_e, n_used, wg, wu, wd, layer):
    p, d = xb.shape
    ff = wg.shape[-1]
    bm = EXPERT_BLOCK
    w_in_spec = pl.BlockSpec((None, None, d, ff), lambda i, be, nu: (layer, be[i], 0, 0))
    grid_spec = pltpu.PrefetchScalarGridSpec(
        num_scalar_prefetch=2,
        grid=(p // bm,),
        in_specs=[
            pl.BlockSpec((bm, d), lambda i, be, nu: (i, 0)),
            w_in_spec, w_in_spec,
            pl.BlockSpec((None, None, ff, d), lambda i, be, nu: (layer, be[i], 0, 0)),
        ],
        out_specs=pl.BlockSpec((bm, d), lambda i, be, nu: (i, 0)),
    )
    return pl.pallas_call(
        _moe_kernel, grid_spec=grid_spec,
        out_shape=jax.ShapeDtypeStruct((p, d), BF16),
        compiler_params=_compiler_params(("arbitrary",)),
        name="expert_mlp",
    )(blk_e, n_used, xb, wg, wu, wd)


def _dispatch_plan(expert, n_tokens):
    bm = EXPERT_BLOCK
    a = n_tokens * TOP_K
    n_blk = (a + N_EXPERTS * (bm - 1)) // bm + 1
    flat_e = expert.reshape(-1)
    onehot = (flat_e[:, None] == jnp.arange(N_EXPERTS, dtype=jnp.int32)[None, :]).astype(jnp.int32)
    csum = jnp.cumsum(onehot, axis=0)
    rank = jnp.take_along_axis(csum, flat_e[:, None], axis=1)[:, 0] - 1
    counts = csum[-1]
    padded = (counts + bm - 1) // bm * bm
    pend = jnp.cumsum(padded)
    dest = (pend - padded)[flat_e] + rank
    row_src = jnp.zeros((n_blk * bm,), jnp.int32).at[dest].set(jnp.arange(a, dtype=jnp.int32) // TOP_K)
    blk_start = jnp.arange(n_blk, dtype=jnp.int32) * bm
    blk_e = jnp.minimum(jnp.sum(blk_start[:, None] >= pend[None, :], axis=1), N_EXPERTS - 1).astype(jnp.int32)
    n_used = (pend[-1] // bm).astype(jnp.int32).reshape(1)
    return dest.reshape(n_tokens, TOP_K), row_src, blk_e, n_used


def _moe(h2, route, wg, wu, wd, layer):
    n_tokens = h2.shape[0]
    expert = route[:, :TOP_K].astype(jnp.int32)
    dest, row_src, blk_e, n_used = _dispatch_plan(expert, n_tokens)
    xb = jnp.take(h2, row_src, axis=0)
    yb = _expert_mlp(xb, blk_e, n_used, wg, wu, wd, layer)
    return tuple(jnp.take(yb, dest[:, k], axis=0) for k in range(TOP_K))


def _final_kernel(x_ref, y0_ref, y1_ref, gate_ref, mod_ref, g_ref, o_ref):
    x = x_ref[...] + mod_ref[5:6, :] * _combine(y0_ref, y1_ref, gate_ref)
    o_ref[...] = (x * lax.rsqrt(jnp.mean(x * x, axis=-1, keepdims=True) + EPS)) * g_ref[...]


def _final(x, y2, route, mod_all, layer, gain, n_rows, mod_row):
    d = x.shape[1]
    tm = TOKEN_TILE
    row_spec = pl.BlockSpec((tm, d), lambda i: (i, 0))
    return pl.pallas_call(
        _final_kernel,
        grid=(n_rows // tm,),
        in_specs=[row_spec, row_spec, row_spec,
                  pl.BlockSpec((tm, V7X_LANES), lambda i: (i, 0)),
                  pl.BlockSpec((None, None, V7X_SUBLANES, d), lambda i: (layer, mod_row(i), 0, 0)),
                  pl.BlockSpec((1, d), lambda i: (0, 0))],
        out_specs=row_spec,
        out_shape=jax.ShapeDtypeStruct((n_rows, d), F32),
        compiler_params=_compiler_params(("arbitrary",)),
        name="final_norm",
    )(x, *y2, route, mod_all, gain.reshape(1, d))


def kernel(x, c, ctx, c_ctx, w_mod, b_mod, norm1, norm2, w_in, rpb, w_pool, pool_scale, w_out,
           w_router, router_bias, w_gate, w_up, w_down, norm_final):
    n_batch, seq, d = x.shape
    ctx_len = ctx.shape[1]
    depth = w_mod.shape[0]
    n_lat = n_batch * seq
    n_ctx = n_batch * ctx_len
    nt = n_lat + n_ctx
    tm = TOKEN_TILE
    assert seq % tm == 0 and n_ctx % tm == 0 and seq % (GRID_W * ATTN_Q_ROWS) == 0
    assert seq % ctx_len == 0 and ctx_len % POOL_TILE == 0 and d == 2 * N_HEADS * HEAD_DIM

    mod_rows = _round_up(n_batch + 1, V7X_SUBLANES)
    cc = jnp.zeros((mod_rows, d), F32).at[:n_batch].set(c).at[n_batch].set(c_ctx)
    mod_all = _modulation(cc, w_mod, b_mod).reshape(depth, mod_rows, N_MOD, d)
    mod_all = jnp.pad(mod_all, ((0, 0), (0, 0), (0, V7X_SUBLANES - N_MOD), (0, 0)))
    n_lat_tiles = n_lat // tm
    mod_row = lambda i: jnp.where(i < n_lat_tiles, i // (seq // tm), n_batch)

    lanes = np.arange(N_EXPERTS)
    perm = (lanes % N_EXPERT_GROUPS) * EXPERTS_PER_GROUP + lanes // N_EXPERT_GROUPS
    w_router_perm = jnp.zeros((d, V7X_LANES), BF16).at[:, :N_EXPERTS].set(w_router[:, perm].astype(BF16))
    rb_perm = jnp.zeros((1, V7X_LANES), F32).at[0, :N_EXPERTS].set(router_bias[perm].astype(F32))

    w_in_b, w_out_b, w_pool_b = w_in.astype(BF16), w_out.astype(BF16), w_pool.astype(BF16)
    w_gate_b, w_up_b, w_down_b = w_gate.astype(BF16), w_up.astype(BF16), w_down.astype(BF16)
    tables = _attn_bias_tables(rpb, seq // GRID_W)

    xs = (x.reshape(n_lat, d), ctx.reshape(n_ctx, d))
    y2 = route = None
    for l in range(depth):
        last = l == depth - 1
        n_rows = n_lat if last else nt
        if l == 0:
            qkvu = _first_proj(xs[0], xs[1], mod_all, l, norm1[l], w_in_b, mod_row)
        else:
            xs, qkvu = _next_proj(xs, y2, route, mod_all, l, norm1[l], w_in_b, mod_row)
        attn = _attention(qkvu, tables, l, n_batch, seq, ctx_len, n_rows)
        pool = _pool(None, qkvu, w_pool_b, l, pool_scale[l], n_batch, seq, 0, n_rows)
        if not last:
            attn = _context_attention(attn, qkvu, n_batch, seq, ctx_len)
            pool = _pool(pool, qkvu, w_pool_b, l, pool_scale[l], n_batch, ctx_len, n_lat, n_rows)
        xs, h2, route = _out_router(attn, pool, xs, mod_all, l, norm2[l], w_out_b,
                                    w_router_perm, rb_perm, n_rows, mod_row)
        y2 = _moe(h2, route, w_gate_b, w_up_b, w_down_b, l)
    out = _final(xs, y2, route, mod_all, depth - 1, norm_final, n_lat, mod_row)
    return out.reshape(n_batch, seq, d)
```

```python
import functools

import numpy as np
import jax
import jax.numpy as jnp
from jax import lax
from jax.experimental import pallas as pl
from jax.experimental.pallas import tpu as pltpu

N_HEADS = 8
HEAD_DIM = 128
GRID_W = 64
WIN_ROWS = 8
WIN_COLS = 16
POOL_WINDOWS = (2, 4, 8, 16)
N_EXPERTS = 32
N_EXPERT_GROUPS = 8
EXPERTS_PER_GROUP = 4
TOP_K = 2
N_MOD = 6
EPS = 1e-6

V7X_LANES = 128
V7X_SUBLANES = 8
V7X_VMEM_LIMIT_BYTES = 56 * 1024 * 1024

TOKEN_TILE = 512
PROJ_CHUNK = 512
MOD_COL_TILE = 1024
ATTN_Q_ROWS = 4
ATTN_K_ROWS = 12
POOL_TILE = 256
EXPERT_BLOCK = 256
FF_CHUNK = 512
MASK_VALUE = -1e30

F32 = jnp.float32
BF16 = jnp.bfloat16


def _round_up(a, b):
    return (a + b - 1) // b * b


def _compiler_params(semantics):
    return pltpu.CompilerParams(dimension_semantics=semantics,
                                vmem_limit_bytes=V7X_VMEM_LIMIT_BYTES)


def _mod_kernel(a_ref, w_ref, b_ref, o_ref):
    a = a_ref[...]
    a = (a * jax.nn.sigmoid(a)).astype(BF16)
    o_ref[...] = jnp.dot(a, w_ref[...].astype(BF16), preferred_element_type=F32) + b_ref[...]


def _modulation(cc, w_mod, b_mod):
    depth, d, n = w_mod.shape
    rows = cc.shape[0]
    tn = MOD_COL_TILE
    return pl.pallas_call(
        _mod_kernel,
        grid=(depth, n // tn),
        in_specs=[
            pl.BlockSpec((rows, d), lambda l, j: (0, 0)),
            pl.BlockSpec((None, d, tn), lambda l, j: (l, 0, j)),
            pl.BlockSpec((None, 1, tn), lambda l, j: (l, 0, j)),
        ],
        out_specs=pl.BlockSpec((None, rows, tn), lambda l, j: (l, 0, j)),
        out_shape=jax.ShapeDtypeStruct((depth, rows, n), F32),
        compiler_params=_compiler_params(("arbitrary", "arbitrary")),
        name="modulation",
    )(cc, w_mod, b_mod.reshape(depth, 1, n))


def _rms_modulate(x, gain, shift, scale):
    y = x * lax.rsqrt(jnp.mean(x * x, axis=-1, keepdims=True) + EPS)
    return (y * gain) * (1.0 + scale) + shift


def _project(hb, w_ref, o_ref, q_cols, q_scale):
    for n0 in range(0, o_ref.shape[1], PROJ_CHUNK):
        acc = jnp.dot(hb, w_ref[:, n0:n0 + PROJ_CHUNK], preferred_element_type=F32)
        if n0 < q_cols:
            acc = acc * q_scale
        o_ref[:, n0:n0 + PROJ_CHUNK] = acc.astype(o_ref.dtype)


def _first_proj_kernel(xl_ref, xc_ref, mod_ref, g_ref, w_ref, o_ref, *, n_lat_tiles, q_cols, q_scale):
    x = jnp.where(pl.program_id(0) < n_lat_tiles, xl_ref[...], xc_ref[...])
    hb = _rms_modulate(x, g_ref[...], mod_ref[0:1, :], mod_ref[1:2, :]).astype(BF16)
    _project(hb, w_ref, o_ref, q_cols, q_scale)


def _next_proj_kernel(x_ref, y0_ref, y1_ref, gate_ref, modp_ref, mod_ref, g_ref, w_ref, xo_ref, o_ref, *,
                      q_cols, q_scale):
    x = x_ref[...] + modp_ref[5:6, :] * _combine(y0_ref, y1_ref, gate_ref)
    xo_ref[...] = x
    hb = _rms_modulate(x, g_ref[...], mod_ref[0:1, :], mod_ref[1:2, :]).astype(BF16)
    _project(hb, w_ref, o_ref, q_cols, q_scale)


def _combine(y0_ref, y1_ref, gate_ref):
    g = gate_ref[...]
    return g[:, 2:3] * y0_ref[...].astype(F32) + g[:, 3:4] * y1_ref[...].astype(F32)


def _split_rows_specs(tm, d, n_lat_tiles):
    return (pl.BlockSpec((tm, d), lambda i: (jnp.minimum(i, n_lat_tiles - 1), 0)),
            pl.BlockSpec((tm, d), lambda i: (jnp.maximum(i - n_lat_tiles, 0), 0)))


def _first_proj(x_lat, x_ctx, mod_all, layer, gain, w_bf16, mod_row):
    n_lat, d = x_lat.shape
    nt = n_lat + x_ctx.shape[0]
    n_out = w_bf16.shape[-1]
    tm = TOKEN_TILE
    n_lat_tiles = n_lat // tm
    kern = functools.partial(_first_proj_kernel, n_lat_tiles=n_lat_tiles, q_cols=N_HEADS * HEAD_DIM,
                             q_scale=HEAD_DIM ** -0.5)
    return pl.pallas_call(
        kern, grid=(nt // tm,),
        in_specs=[
            *_split_rows_specs(tm, d, n_lat_tiles),
            pl.BlockSpec((None, None, V7X_SUBLANES, d), lambda i: (layer, mod_row(i), 0, 0)),
            pl.BlockSpec((1, d), lambda i: (0, 0)),
            pl.BlockSpec((None, d, n_out), lambda i: (layer, 0, 0), pipeline_mode=pl.Buffered(1)),
        ],
        out_specs=pl.BlockSpec((tm, n_out), lambda i: (i, 0)),
        out_shape=jax.ShapeDtypeStruct((nt, n_out), BF16),
        compiler_params=_compiler_params(("arbitrary",)),
        name="norm_in_proj_first",
    )(x_lat, x_ctx, mod_all, gain.reshape(1, d), w_bf16)


def _next_proj(x, y2, route, mod_all, layer, gain, w_bf16, mod_row):
    nt, d = x.shape
    n_out = w_bf16.shape[-1]
    tm = TOKEN_TILE
    row_spec = pl.BlockSpec((tm, d), lambda i: (i, 0))
    mod_spec = lambda l: pl.BlockSpec((None, None, V7X_SUBLANES, d), lambda i: (l, mod_row(i), 0, 0))
    kern = functools.partial(_next_proj_kernel, q_cols=N_HEADS * HEAD_DIM, q_scale=HEAD_DIM ** -0.5)
    return pl.pallas_call(
        kern, grid=(nt // tm,),
        in_specs=[
            row_spec, row_spec, row_spec,
            pl.BlockSpec((tm, V7X_LANES), lambda i: (i, 0)),
            mod_spec(layer - 1), mod_spec(layer),
            pl.BlockSpec((1, d), lambda i: (0, 0)),
            pl.BlockSpec((None, d, n_out), lambda i: (layer, 0, 0), pipeline_mode=pl.Buffered(1)),
        ],
        out_specs=[row_spec, pl.BlockSpec((tm, n_out), lambda i: (i, 0))],
        out_shape=[jax.ShapeDtypeStruct((nt, d), F32), jax.ShapeDtypeStruct((nt, n_out), BF16)],
        compiler_params=_compiler_params(("arbitrary",)),
        name="norm_in_proj_next",
    )(x, *y2, route, mod_all, mod_all, gain.reshape(1, d), w_bf16)


def _attn_window_start(gi, n_rows):
    return int(np.clip(gi * ATTN_Q_ROWS - WIN_ROWS // 2, 0, n_rows - ATTN_K_ROWS))


def _attn_kernel(q_ref, k_ref, v_ref, kc_ref, vc_ref, tab_ref, o_ref, *, n_rows):
    gq = ATTN_Q_ROWS * GRID_W
    wk = ATTN_K_ROWS * GRID_W
    n_groups = n_rows // ATTN_Q_ROWS
    kc = kc_ref[...]
    vc = vc_ref[...]
    nt_dims = (((1,), (1,)), ((), ()))
    for gi in range(n_groups):
        q0 = gi * gq
        k0 = _attn_window_start(gi, n_rows) * GRID_W
        kind = 0 if gi == 0 else (2 if gi == n_groups - 1 else 1)
        q = q_ref[q0:q0 + gq, :]
        kw = k_ref[k0:k0 + wk, :]
        vw = v_ref[k0:k0 + wk, :]
        s_loc = lax.dot_general(q, kw, nt_dims, preferred_element_type=F32) + tab_ref[kind]
        s_ctx = lax.dot_general(q, kc, nt_dims, preferred_element_type=F32)
        m = jnp.maximum(jnp.max(s_loc, axis=1, keepdims=True), jnp.max(s_ctx, axis=1, keepdims=True))
        p_loc = jnp.exp(s_loc - m)
        p_ctx = jnp.exp(s_ctx - m)
        den = jnp.sum(p_loc, axis=1, keepdims=True) + jnp.sum(p_ctx, axis=1, keepdims=True)
        o = (jnp.dot(p_loc.astype(BF16), vw, preferred_element_type=F32)
             + jnp.dot(p_ctx.astype(BF16), vc, preferred_element_type=F32))
        o_ref[q0:q0 + gq, :] = (o / den).astype(o_ref.dtype)


def _attn_bias_tables(rpb, n_rows):
    n_groups = n_rows // ATTN_Q_ROWS
    kr_n = min(WIN_ROWS, n_rows)

    def row_pattern(gi):
        ws = _attn_window_start(gi, n_rows)
        r = gi * ATTN_Q_ROWS + np.arange(ATTN_Q_ROWS)[:, None]
        kr = ws + np.arange(ATTN_K_ROWS)[None, :]
        r0 = np.clip(r - kr_n // 2, 0, n_rows - kr_n)
        valid = (kr >= r0) & (kr < r0 + kr_n)
        return valid, np.clip(kr - r + WIN_ROWS - 1, 0, 2 * WIN_ROWS - 2)

    pats = [row_pattern(g) for g in range(n_groups)]
    for g in range(2, n_groups - 1):
        assert all(np.array_equal(a, b) for a, b in zip(pats[1], pats[g]))
    kinds = (0, 1, n_groups - 1)
    valid_r = np.stack([pats[g][0] for g in kinds])
    dr = np.stack([pats[g][1] for g in kinds])
    qc = np.arange(GRID_W)[:, None]
    kc = np.arange(GRID_W)[None, :]
    c0 = np.clip(qc - WIN_COLS // 2, 0, GRID_W - WIN_COLS)
    valid_c = (kc >= c0) & (kc < c0 + WIN_COLS)
    dc = np.clip(kc - qc + WIN_COLS - 1, 0, 2 * WIN_COLS - 2)
    rsel = jnp.asarray(np.eye(2 * WIN_ROWS - 1, dtype=np.float32)[dr])
    csel = jnp.asarray(np.eye(2 * WIN_COLS - 1, dtype=np.float32)[dc])
    a = jnp.einsum("tqkr,lhrc->lhtqkc", rsel, rpb.astype(F32), precision=lax.Precision.HIGHEST)
    t = jnp.einsum("lhtqkc,pjc->lhtqpkj", a, csel, precision=lax.Precision.HIGHEST)
    valid = valid_r[:, :, None, :, None] & valid_c[None, None, :, None, :]
    t = jnp.where(jnp.asarray(valid)[None, None], t, MASK_VALUE)
    return t.reshape(t.shape[:3] + (ATTN_Q_ROWS * GRID_W, ATTN_K_ROWS * GRID_W))


def _attention(qkvu, tables, layer, n_batch, seq, ctx_len):
    attn_w = N_HEADS * HEAD_DIM
    n_rows_out = n_batch * seq
    ctx_blk0 = (n_batch * seq) // ctx_len
    kern = functools.partial(_attn_kernel, n_rows=seq // GRID_W)
    tab_shape = tables.shape[2:]
    return pl.pallas_call(
        kern,
        grid=(n_batch, N_HEADS),
        in_specs=[
            pl.BlockSpec((seq, HEAD_DIM), lambda b, h: (b, h)),
            pl.BlockSpec((seq, HEAD_DIM), lambda b, h: (b, N_HEADS + h)),
            pl.BlockSpec((seq, HEAD_DIM), lambda b, h: (b, 2 * N_HEADS + h)),
            pl.BlockSpec((ctx_len, HEAD_DIM), lambda b, h: (ctx_blk0 + b, N_HEADS + h)),
            pl.BlockSpec((ctx_len, HEAD_DIM), lambda b, h: (ctx_blk0 + b, 2 * N_HEADS + h)),
            pl.BlockSpec((None, None) + tab_shape, lambda b, h: (layer, h, 0, 0, 0)),
        ],
        out_specs=pl.BlockSpec((seq, HEAD_DIM), lambda b, h: (b, h)),
        out_shape=jax.ShapeDtypeStruct((n_rows_out, attn_w), BF16),
        compiler_params=_compiler_params(("arbitrary", "arbitrary")),
        name="neighbourhood_attention",
    )(qkvu, qkvu, qkvu, qkvu, qkvu, tables)


def _ctx_attn_kernel(q_ref, k_ref, v_ref, o_ref):
    s = lax.dot_general(q_ref[...], k_ref[...], (((1,), (1,)), ((), ())), preferred_element_type=F32)
    m = jnp.max(s, axis=1, keepdims=True)
    p = jnp.exp(s - m)
    den = jnp.sum(p, axis=1, keepdims=True)
    o = jnp.dot(p.astype(BF16), v_ref[...], preferred_element_type=F32)
    o_ref[...] = (o / den).astype(o_ref.dtype)


def _context_attention(qkvu, n_batch, seq, ctx_len):
    blk0 = (n_batch * seq) // ctx_len
    return pl.pallas_call(
        _ctx_attn_kernel,
        grid=(n_batch, N_HEADS),
        in_specs=[
            pl.BlockSpec((ctx_len, HEAD_DIM), lambda b, h: (blk0 + b, h)),
            pl.BlockSpec((ctx_len, HEAD_DIM), lambda b, h: (blk0 + b, N_HEADS + h)),
            pl.BlockSpec((ctx_len, HEAD_DIM), lambda b, h: (blk0 + b, 2 * N_HEADS + h)),
        ],
        out_specs=pl.BlockSpec((ctx_len, HEAD_DIM), lambda b, h: (b, h)),
        out_shape=jax.ShapeDtypeStruct((n_batch * ctx_len, N_HEADS * HEAD_DIM), BF16),
        compiler_params=_compiler_params(("arbitrary", "arbitrary")),
        name="context_attention",
    )(qkvu, qkvu, qkvu)


def _pool_plan(seq):
    t = POOL_TILE
    win = min(2 * t, seq)
    n_tiles = seq // t
    starts = [int(np.clip(j * t - (win - t) // 2, 0, seq - win)) for j in range(n_tiles)]

    def band(w, j):
        left, right = w // 2, w - 1 - w // 2
        pos = j * t + np.arange(t)[:, None]
        src = starts[j] + np.arange(win)[None, :]
        return ((src >= np.maximum(pos - left, 0)) & (src < np.minimum(pos + right + 1, seq))).astype(np.float32)

    kind_tiles = sorted({0, min(1, n_tiles - 1), n_tiles - 1})
    kind_of = [0 if j == 0 else (len(kind_tiles) - 1 if j == n_tiles - 1 else 1) for j in range(n_tiles)]
    for w in POOL_WINDOWS:
        for j in range(n_tiles):
            assert np.array_equal(band(w, kind_tiles[kind_of[j]]), band(w, j))
    bands = np.stack([np.stack([band(w, j) for j in kind_tiles]) for w in POOL_WINDOWS])
    return jnp.asarray(bands, BF16), win, list(zip(starts, kind_of))


def _pool_kernel(u_ref, band_ref, w_ref, sc_ref, o_ref, *, seq, win, tiles):
    t = POOL_TILE
    left = jnp.left_shift(1, pl.program_id(1))
    w = w_ref[...]
    sc = sc_ref[...]
    for j, (ws, kind) in enumerate(tiles):
        t0 = j * t
        total = jnp.dot(band_ref[kind], u_ref[ws:ws + win, :], preferred_element_type=F32)
        pos = t0 + lax.broadcasted_iota(jnp.int32, total.shape, 0)
        cnt = jnp.minimum(pos + left, seq) - jnp.maximum(pos - left, 0)
        pooled = (total / cnt.astype(F32) - u_ref[t0:t0 + t, :].astype(F32)).astype(BF16)
        y = jnp.dot(pooled, w, preferred_element_type=F32)
        o_ref[t0:t0 + t, :] = (y * sc).astype(o_ref.dtype)


def _pool(qkvu, w_pool_bf16, layer, pool_scale, n_seqs, seq, row0):
    n_g = len(POOL_WINDOWS)
    assert POOL_WINDOWS == tuple(2 << g for g in range(n_g))
    cg = w_pool_bf16.shape[-1]
    u_col0 = (3 * N_HEADS * HEAD_DIM) // cg
    bands, win, tiles = _pool_plan(seq)
    seq_blk0 = row0 // seq
    return pl.pallas_call(
        functools.partial(_pool_kernel, seq=seq, win=win, tiles=tiles),
        grid=(n_seqs, n_g),
        in_specs=[
            pl.BlockSpec((seq, cg), lambda b, g: (seq_blk0 + b, u_col0 + g)),
            pl.BlockSpec((None,) + bands.shape[1:], lambda b, g: (g, 0, 0, 0)),
            pl.BlockSpec((None, None, cg, cg), lambda b, g: (layer, g, 0, 0)),
            pl.BlockSpec((1, cg), lambda b, g: (0, g)),
        ],
        out_specs=pl.BlockSpec((seq, cg), lambda b, g: (b, g)),
        out_shape=jax.ShapeDtypeStruct((n_seqs * seq, n_g * cg), BF16),
        compiler_params=_compiler_params(("arbitrary", "arbitrary")),
        name="multiscale_pool",
    )(qkvu, bands, w_pool_bf16, pool_scale.reshape(1, n_g * cg))


def _first_max_index(vals):
    best = vals[0]
    for v in vals[1:]:
        best = jnp.maximum(best, v)
    idx = jnp.full(best.shape, len(vals), jnp.int32)
    for j in reversed(range(len(vals))):
        idx = jnp.where(vals[j] == best, j, idx)
    return best, idx


def _route(hb, wr_ref, rb_ref):
    logits = jnp.dot(hb, wr_ref[...], preferred_element_type=F32)
    s = jax.nn.sigmoid(logits)
    sb = s + rb_ref[...]
    lane = lax.broadcasted_iota(jnp.int32, sb.shape, 1)
    n_g = N_EXPERT_GROUPS

    def member(arr, j):
        return arr if j == 0 else pltpu.roll(arr, V7X_LANES - n_g * j, axis=1)

    sb_j = [member(sb, j) for j in range(EXPERTS_PER_GROUP)]
    s_j = [member(s, j) for j in range(EXPERTS_PER_GROUP)]
    hi1, lo1 = jnp.maximum(sb_j[0], sb_j[1]), jnp.minimum(sb_j[0], sb_j[1])
    hi2, lo2 = jnp.maximum(sb_j[2], sb_j[3]), jnp.minimum(sb_j[2], sb_j[3])
    top1 = jnp.maximum(hi1, hi2)
    top2 = jnp.maximum(jnp.minimum(hi1, hi2), jnp.maximum(lo1, lo2))
    grp = jnp.where(lane < n_g, top1 + top2, -jnp.inf)
    gmax = jnp.max(grp, axis=1, keepdims=True)
    lane_f = lane.astype(F32)
    gidx = jnp.min(jnp.where(grp == gmax, lane_f, float(V7X_LANES)), axis=1, keepdims=True)
    sel = lane_f == gidx
    sbv = [jnp.sum(jnp.where(sel, v, 0.0), axis=1, keepdims=True) for v in sb_j]
    sv = [jnp.sum(jnp.where(sel, v, 0.0), axis=1, keepdims=True) for v in s_j]
    _, loc0 = _first_max_index(sbv)
    _, loc1 = _first_max_index([jnp.where(loc0 == j, -jnp.inf, sbv[j]) for j in range(EXPERTS_PER_GROUP)])
    pick = lambda loc: sum(jnp.where(loc == j, sv[j], 0.0) for j in range(EXPERTS_PER_GROUP))
    s0, s1 = pick(loc0), pick(loc1)
    tot = s0 + s1
    e0 = gidx * EXPERTS_PER_GROUP + loc0.astype(F32)
    e1 = gidx * EXPERTS_PER_GROUP + loc1.astype(F32)
    return jnp.where(lane == 0, e0, jnp.where(lane == 1, e1,
                     jnp.where(lane == 2, s0 / tot, jnp.where(lane == 3, s1 / tot, 0.0))))


def _out_router_kernel(*refs, n_lat_tiles, split):
    refs = list(refs)

    def rows(is_split):
        if not is_split:
            return refs.pop(0)[...]
        lat, ctx = refs.pop(0), refs.pop(0)
        return jnp.where(pl.program_id(0) < n_lat_tiles, lat[...], ctx[...])

    a, p, x = (rows(s) for s in split)
    mod_ref, g_ref, wo_ref, wr_ref, rb_ref, xo_ref, h_ref, r_ref = refs
    aw = a.shape[1]
    mix = (jnp.dot(a, wo_ref[0:aw, :], preferred_element_type=F32)
           + jnp.dot(p, wo_ref[aw:, :], preferred_element_type=F32))
    x = x + mod_ref[2:3, :] * mix
    xo_ref[...] = x
    hb = _rms_modulate(x, g_ref[...], mod_ref[3:4, :], mod_ref[4:5, :]).astype(BF16)
    h_ref[...] = _pack_bf16_pairs(hb)
    r_ref[...] = _route(hb, wr_ref, rb_ref)


def _out_router(attn, pool, xs, mod_all, layer, gain, w_out_bf16, w_router_perm, rb_perm, n_rows,
                n_lat_tiles, mod_row):
    tm = TOKEN_TILE
    row = lambda w: pl.BlockSpec((tm, w), lambda i: (i, 0))
    operands, specs, split, widths = [], [], [], []
    for op in (attn, pool, xs):
        split.append(isinstance(op, tuple))
        parts = op if split[-1] else (op,)
        widths.append(parts[0].shape[1])
        operands += parts
        specs += list(_split_rows_specs(tm, widths[-1], n_lat_tiles)) if split[-1] else [row(widths[-1])]
    aw, pw, d = widths
    return pl.pallas_call(
        functools.partial(_out_router_kernel, n_lat_tiles=n_lat_tiles, split=tuple(split)),
        grid=(n_rows // tm,),
        in_specs=[
            *specs,
            pl.BlockSpec((None, None, V7X_SUBLANES, d), lambda i: (layer, mod_row(i), 0, 0)),
            pl.BlockSpec((1, d), lambda i: (0, 0)),
            pl.BlockSpec((None, aw + pw, d), lambda i: (layer, 0, 0), pipeline_mode=pl.Buffered(1)),
            pl.BlockSpec((d, V7X_LANES), lambda i: (0, 0)),
            pl.BlockSpec((1, V7X_LANES), lambda i: (0, 0)),
        ],
        out_specs=[row(d), row(d // 2), row(V7X_LANES)],
        out_shape=[jax.ShapeDtypeStruct((n_rows, d), F32),
                   jax.ShapeDtypeStruct((n_rows, d // 2), jnp.uint32),
                   jax.ShapeDtypeStruct((n_rows, V7X_LANES), F32)],
        compiler_params=_compiler_params(("arbitrary",)),
        name="out_proj_router",
    )(*operands, mod_all, gain.reshape(1, d), w_out_bf16, w_router_perm, rb_perm)


def _pack_bf16_pairs(hb):
    k = hb.shape[1] // 2
    hi = lax.bitcast_convert_type(hb[:, :k].astype(F32), jnp.uint32)
    lo = lax.bitcast_convert_type(hb[:, k:].astype(F32), jnp.uint32)
    return hi | (lo >> 16)


def _unpack_bf16_pairs(w):
    hi = lax.bitcast_convert_type(w & jnp.uint32(0xFFFF0000), F32).astype(BF16)
    lo = lax.bitcast_convert_type(w << 16, F32).astype(BF16)
    return hi, lo


def _moe_kernel(blk_e_ref, n_used_ref, src_ref, h_hbm, wg_ref, wu_ref, wd_ref, o_ref,
                xbuf0, xbuf1, sem0, sem1):
    del blk_e_ref
    i = pl.program_id(0)
    n_used = n_used_ref[0]
    bm = EXPERT_BLOCK
    slots = ((xbuf0, sem0), (xbuf1, sem1))

    def start_gather(blk, slot):
        buf, sem = slots[slot]
        for r in range(bm):
            pltpu.make_async_copy(h_hbm.at[pl.ds(src_ref[blk * bm + r], 1), :],
                                  buf.at[pl.ds(r, 1), :], sem).start()

    def wait_gather(slot):
        buf, sem = slots[slot]
        pltpu.make_async_copy(h_hbm.at[pl.ds(0, bm), :], buf, sem).wait()

    @pl.when(i == 0)
    def _():
        start_gather(0, 0)

    for parity in (0, 1):
        @pl.when((i % 2 == parity) & (i == n_used))
        def _():
            wait_gather(parity)

        @pl.when((i % 2 == parity) & (i < n_used))
        def _():
            wait_gather(parity)
            start_gather(i + 1, 1 - parity)
            _expert_block(slots[parity][0], wg_ref, wu_ref, wd_ref, o_ref)

    @pl.when(i >= n_used)
    def _():
        o_ref[...] = jnp.zeros(o_ref.shape, o_ref.dtype)


def _expert_block(x_ref, wg_ref, wu_ref, wd_ref, o_ref):
    xa, xb = _unpack_bf16_pairs(x_ref[...])
    k = xa.shape[1]
    ff = wg_ref.shape[1]
    acc = None
    for f0 in range(0, ff, FF_CHUNK):
        cols = slice(f0, f0 + FF_CHUNK)
        g = (jnp.dot(xa, wg_ref[:k, cols], preferred_element_type=F32)
             + jnp.dot(xb, wg_ref[k:, cols], preferred_element_type=F32))
        u = (jnp.dot(xa, wu_ref[:k, cols], preferred_element_type=F32)
             + jnp.dot(xb, wu_ref[k:, cols], preferred_element_type=F32))
        a = ((g * jax.nn.sigmoid(g)) * u).astype(BF16)
        part = jnp.dot(a, wd_ref[cols, :], preferred_element_type=F32)
        acc = part if acc is None else acc + part
    o_ref[...] = acc.astype(o_ref.dtype)


def _expert_mlp(h2p, row_src, blk_e, n_used, wg, wu, wd, layer):
    bm = EXPERT_BLOCK
    n_blk = row_src.shape[0] // bm - 1
    d, ff = wg.shape[-2:]
    w_in_spec = pl.BlockSpec((None, None, d, ff), lambda i, be, nu, src: (layer, be[i], 0, 0))
    grid_spec = pltpu.PrefetchScalarGridSpec(
        num_scalar_prefetch=3,
        grid=(n_blk,),
        in_specs=[
            pl.BlockSpec(memory_space=pl.ANY),
            w_in_spec, w_in_spec,
            pl.BlockSpec((None, None, ff, d), lambda i, be, nu, src: (layer, be[i], 0, 0)),
        ],
        out_specs=pl.BlockSpec((bm, d), lambda i, be, nu, src: (i, 0)),
        scratch_shapes=[pltpu.VMEM((bm, h2p.shape[1]), h2p.dtype)] * 2 + [pltpu.SemaphoreType.DMA(())] * 2,
    )
    return pl.pallas_call(
        _moe_kernel, grid_spec=grid_spec,
        out_shape=jax.ShapeDtypeStruct((n_blk * bm, d), BF16),
        compiler_params=_compiler_params(("arbitrary",)),
        name="expert_mlp",
    )(blk_e, n_used, row_src, h2p, wg, wu, wd)


def _dispatch_plan(expert, n_tokens):
    bm = EXPERT_BLOCK
    a = n_tokens * TOP_K
    n_blk = (a + N_EXPERTS * (bm - 1)) // bm + 1
    flat_e = expert.reshape(-1)
    onehot = (flat_e[:, None] == jnp.arange(N_EXPERTS, dtype=jnp.int32)[None, :]).astype(jnp.int32)
    csum = jnp.cumsum(onehot, axis=0)
    rank = jnp.take_along_axis(csum, flat_e[:, None], axis=1)[:, 0] - 1
    counts = csum[-1]
    padded = (counts + bm - 1) // bm * bm
    pend = jnp.cumsum(padded)
    dest = (pend - padded)[flat_e] + rank
    row_src = jnp.zeros(((n_blk + 1) * bm,), jnp.int32).at[dest].set(
        jnp.arange(a, dtype=jnp.int32) // TOP_K, unique_indices=True, mode="promise_in_bounds")
    blk_start = jnp.arange(n_blk, dtype=jnp.int32) * bm
    blk_e = jnp.minimum(jnp.sum(blk_start[:, None] >= pend[None, :], axis=1), N_EXPERTS - 1).astype(jnp.int32)
    n_used = (pend[-1] // bm).astype(jnp.int32).reshape(1)
    return dest.reshape(n_tokens, TOP_K), row_src, blk_e, n_used


def _moe(h2p, route, wg, wu, wd, layer):
    n_tokens = h2p.shape[0]
    expert = route[:, :TOP_K].astype(jnp.int32)
    dest, row_src, blk_e, n_used = _dispatch_plan(expert, n_tokens)
    yb = _expert_mlp(h2p, row_src, blk_e, n_used, wg, wu, wd, layer)
    return tuple(yb.at[dest[:, k]].get(mode="promise_in_bounds") for k in range(TOP_K))


def _final_kernel(x_ref, y0_ref, y1_ref, gate_ref, mod_ref, g_ref, o_ref):
    x = x_ref[...] + mod_ref[5:6, :] * _combine(y0_ref, y1_ref, gate_ref)
    o_ref[...] = (x * lax.rsqrt(jnp.mean(x * x, axis=-1, keepdims=True) + EPS)) * g_ref[...]


def _final(x, y2, route, mod_all, layer, gain, n_rows, mod_row):
    d = x.shape[1]
    tm = TOKEN_TILE
    row_spec = pl.BlockSpec((tm, d), lambda i: (i, 0))
    return pl.pallas_call(
        _final_kernel,
        grid=(n_rows // tm,),
        in_specs=[row_spec, row_spec, row_spec,
                  pl.BlockSpec((tm, V7X_LANES), lambda i: (i, 0)),
                  pl.BlockSpec((None, None, V7X_SUBLANES, d), lambda i: (layer, mod_row(i), 0, 0)),
                  pl.BlockSpec((1, d), lambda i: (0, 0))],
        out_specs=row_spec,
        out_shape=jax.ShapeDtypeStruct((n_rows, d), F32),
        compiler_params=_compiler_params(("arbitrary",)),
        name="final_norm",
    )(x, *y2, route, mod_all, gain.reshape(1, d))


def kernel(x, c, ctx, c_ctx, w_mod, b_mod, norm1, norm2, w_in, rpb, w_pool, pool_scale, w_out,
           w_router, router_bias, w_gate, w_up, w_down, norm_final):
    n_batch, seq, d = x.shape
    ctx_len = ctx.shape[1]
    depth = w_mod.shape[0]
    n_lat = n_batch * seq
    n_ctx = n_batch * ctx_len
    nt = n_lat + n_ctx
    tm = TOKEN_TILE
    assert seq % tm == 0 and n_ctx % tm == 0 and seq % (GRID_W * ATTN_Q_ROWS) == 0
    assert seq % ctx_len == 0 and ctx_len % POOL_TILE == 0 and d == 2 * N_HEADS * HEAD_DIM

    mod_rows = _round_up(n_batch + 1, V7X_SUBLANES)
    cc = jnp.zeros((mod_rows, d), F32).at[:n_batch].set(c).at[n_batch].set(c_ctx)
    mod_all = _modulation(cc, w_mod, b_mod).reshape(depth, mod_rows, N_MOD, d)
    mod_all = jnp.pad(mod_all, ((0, 0), (0, 0), (0, V7X_SUBLANES - N_MOD), (0, 0)))
    n_lat_tiles = n_lat // tm
    mod_row = lambda i: jnp.where(i < n_lat_tiles, i // (seq // tm), n_batch)

    lanes = np.arange(N_EXPERTS)
    perm = (lanes % N_EXPERT_GROUPS) * EXPERTS_PER_GROUP + lanes // N_EXPERT_GROUPS
    w_router_perm = jnp.zeros((d, V7X_LANES), BF16).at[:, :N_EXPERTS].set(w_router[:, perm].astype(BF16))
    rb_perm = jnp.zeros((1, V7X_LANES), F32).at[0, :N_EXPERTS].set(router_bias[perm].astype(F32))

    w_in_b, w_out_b, w_pool_b = w_in.astype(BF16), w_out.astype(BF16), w_pool.astype(BF16)
    w_gate_b, w_up_b, w_down_b = w_gate.astype(BF16), w_up.astype(BF16), w_down.astype(BF16)
    tables = _attn_bias_tables(rpb, seq // GRID_W)

    xs = (x.reshape(n_lat, d), ctx.reshape(n_ctx, d))
    y2 = route = None
    for l in range(depth):
        last = l == depth - 1
        n_rows = n_lat if last else nt
        if l == 0:
            qkvu = _first_proj(xs[0], xs[1], mod_all, l, norm1[l], w_in_b, mod_row)
        else:
            xs, qkvu = _next_proj(xs, y2, route, mod_all, l, norm1[l], w_in_b, mod_row)
        attn = _attention(qkvu, tables, l, n_batch, seq, ctx_len)
        pool = _pool(qkvu, w_pool_b, l, pool_scale[l], n_batch, seq, 0)
        if not last:
            attn = (attn, _context_attention(qkvu, n_batch, seq, ctx_len))
            pool = (pool, _pool(qkvu, w_pool_b, l, pool_scale[l], n_batch, ctx_len, n_lat))
        xs, h2, route = _out_router(attn, pool, xs, mod_all, l, norm2[l], w_out_b,
                                    w_router_perm, rb_perm, n_rows, n_lat_tiles, mod_row)
        y2 = _moe(h2, route, w_gate_b, w_up_b, w_down_b, l)
    out = _final(xs, y2, route, mod_all, depth - 1, norm_final, n_lat, mod_row)
    return out.reshape(n_batch, seq, d)
```

```python
import functools

import numpy as np
import jax
import jax.numpy as jnp
from jax import lax
from jax.experimental import pallas as pl
from jax.experimental.pallas import tpu as pltpu

N_HEADS = 8
HEAD_DIM = 128
GRID_W = 64
WIN_ROWS = 8
WIN_COLS = 16
POOL_WINDOWS = (2, 4, 8, 16)
N_EXPERTS = 32
N_EXPERT_GROUPS = 8
EXPERTS_PER_GROUP = 4
TOP_K = 2
N_MOD = 6
EPS = 1e-6

V7X_LANES = 128
V7X_SUBLANES = 8
V7X_VMEM_LIMIT_BYTES = 56 * 1024 * 1024

TOKEN_TILE = 512
PROJ_CHUNK = 512
MOD_COL_TILE = 1024
ATTN_Q_ROWS = 4
ATTN_K_ROWS = 12
POOL_TILE = 256
EXPERT_BLOCK = 256
FF_CHUNK = 512
MASK_VALUE = -1e30

F32 = jnp.float32
BF16 = jnp.bfloat16


def _round_up(a, b):
    return (a + b - 1) // b * b


def _compiler_params(semantics):
    return pltpu.CompilerParams(dimension_semantics=semantics,
                                vmem_limit_bytes=V7X_VMEM_LIMIT_BYTES)


def _mod_kernel(a_ref, w_ref, b_ref, o_ref):
    a = a_ref[...]
    a = (a * jax.nn.sigmoid(a)).astype(BF16)
    o_ref[...] = jnp.dot(a, w_ref[...].astype(BF16), preferred_element_type=F32) + b_ref[...]


def _modulation(cc, w_mod, b_mod):
    depth, d, n = w_mod.shape
    rows = cc.shape[0]
    tn = MOD_COL_TILE
    return pl.pallas_call(
        _mod_kernel,
        grid=(depth, n // tn),
        in_specs=[
            pl.BlockSpec((rows, d), lambda l, j: (0, 0)),
            pl.BlockSpec((None, d, tn), lambda l, j: (l, 0, j)),
            pl.BlockSpec((None, 1, tn), lambda l, j: (l, 0, j)),
        ],
        out_specs=pl.BlockSpec((None, rows, tn), lambda l, j: (l, 0, j)),
        out_shape=jax.ShapeDtypeStruct((depth, rows, n), F32),
        compiler_params=_compiler_params(("arbitrary", "arbitrary")),
        name="modulation",
    )(cc, w_mod, b_mod.reshape(depth, 1, n))


def _rms_modulate(x, gain, shift, scale):
    y = x * lax.rsqrt(jnp.mean(x * x, axis=-1, keepdims=True) + EPS)
    return (y * gain) * (1.0 + scale) + shift


def _project(hb, w_ref, o_ref, q_cols, q_scale):
    for n0 in range(0, o_ref.shape[1], PROJ_CHUNK):
        acc = jnp.dot(hb, w_ref[:, n0:n0 + PROJ_CHUNK], preferred_element_type=F32)
        if n0 < q_cols:
            acc = acc * q_scale
        o_ref[:, n0:n0 + PROJ_CHUNK] = acc.astype(o_ref.dtype)


def _first_proj_kernel(xl_ref, xc_ref, mod_ref, g_ref, w_ref, o_ref, *, n_lat_tiles, q_cols, q_scale):
    x = jnp.where(pl.program_id(0) < n_lat_tiles, xl_ref[...], xc_ref[...])
    hb = _rms_modulate(x, g_ref[...], mod_ref[0:1, :], mod_ref[1:2, :]).astype(BF16)
    _project(hb, w_ref, o_ref, q_cols, q_scale)


def _next_proj_kernel(x_ref, y0_ref, y1_ref, gate_ref, modp_ref, mod_ref, g_ref, w_ref, xo_ref, o_ref, *,
                      q_cols, q_scale):
    x = x_ref[...] + modp_ref[5:6, :] * _combine(y0_ref, y1_ref, gate_ref)
    xo_ref[...] = x
    hb = _rms_modulate(x, g_ref[...], mod_ref[0:1, :], mod_ref[1:2, :]).astype(BF16)
    _project(hb, w_ref, o_ref, q_cols, q_scale)


def _combine(y0_ref, y1_ref, gate_ref):
    g = gate_ref[...]
    return g[:, 2:3] * y0_ref[...].astype(F32) + g[:, 3:4] * y1_ref[...].astype(F32)


def _split_rows_specs(tm, d, n_lat_tiles):
    return (pl.BlockSpec((tm, d), lambda i: (jnp.minimum(i, n_lat_tiles - 1), 0)),
            pl.BlockSpec((tm, d), lambda i: (jnp.maximum(i - n_lat_tiles, 0), 0)))


def _first_proj(x_lat, x_ctx, mod_all, layer, gain, w_bf16, mod_row):
    n_lat, d = x_lat.shape
    nt = n_lat + x_ctx.shape[0]
    n_out = w_bf16.shape[-1]
    tm = TOKEN_TILE
    n_lat_tiles = n_lat // tm
    kern = functools.partial(_first_proj_kernel, n_lat_tiles=n_lat_tiles, q_cols=N_HEADS * HEAD_DIM,
                             q_scale=HEAD_DIM ** -0.5)
    return pl.pallas_call(
        kern, grid=(nt // tm,),
        in_specs=[
            *_split_rows_specs(tm, d, n_lat_tiles),
            pl.BlockSpec((None, None, V7X_SUBLANES, d), lambda i: (layer, mod_row(i), 0, 0)),
            pl.BlockSpec((1, d), lambda i: (0, 0)),
            pl.BlockSpec((None, d, n_out), lambda i: (layer, 0, 0), pipeline_mode=pl.Buffered(1)),
        ],
        out_specs=pl.BlockSpec((tm, n_out), lambda i: (i, 0)),
        out_shape=jax.ShapeDtypeStruct((nt, n_out), BF16),
        compiler_params=_compiler_params(("arbitrary",)),
        name="norm_in_proj_first",
    )(x_lat, x_ctx, mod_all, gain.reshape(1, d), w_bf16)


def _next_proj(x, y2, route, mod_all, layer, gain, w_bf16, mod_row):
    nt, d = x.shape
    n_out = w_bf16.shape[-1]
    tm = TOKEN_TILE
    row_spec = pl.BlockSpec((tm, d), lambda i: (i, 0))
    mod_spec = lambda l: pl.BlockSpec((None, None, V7X_SUBLANES, d), lambda i: (l, mod_row(i), 0, 0))
    kern = functools.partial(_next_proj_kernel, q_cols=N_HEADS * HEAD_DIM, q_scale=HEAD_DIM ** -0.5)
    return pl.pallas_call(
        kern, grid=(nt // tm,),
        in_specs=[
            row_spec, row_spec, row_spec,
            pl.BlockSpec((tm, V7X_LANES), lambda i: (i, 0)),
            mod_spec(layer - 1), mod_spec(layer),
            pl.BlockSpec((1, d), lambda i: (0, 0)),
            pl.BlockSpec((None, d, n_out), lambda i: (layer, 0, 0), pipeline_mode=pl.Buffered(1)),
        ],
        out_specs=[row_spec, pl.BlockSpec((tm, n_out), lambda i: (i, 0))],
        out_shape=[jax.ShapeDtypeStruct((nt, d), F32), jax.ShapeDtypeStruct((nt, n_out), BF16)],
        compiler_params=_compiler_params(("arbitrary",)),
        name="norm_in_proj_next",
    )(x, *y2, route, mod_all, mod_all, gain.reshape(1, d), w_bf16)


def _attn_window_start(gi, n_rows):
    return int(np.clip(gi * ATTN_Q_ROWS - WIN_ROWS // 2, 0, n_rows - ATTN_K_ROWS))


def _attn_kernel(q_ref, k_ref, v_ref, kc_ref, vc_ref, tab_ref, o_ref, *, n_rows):
    gq = ATTN_Q_ROWS * GRID_W
    wk = ATTN_K_ROWS * GRID_W
    n_groups = n_rows // ATTN_Q_ROWS
    kc = kc_ref[...]
    vc = vc_ref[...]
    nt_dims = (((1,), (1,)), ((), ()))
    for gi in range(n_groups):
        q0 = gi * gq
        k0 = _attn_window_start(gi, n_rows) * GRID_W
        kind = 0 if gi == 0 else (2 if gi == n_groups - 1 else 1)
        q = q_ref[q0:q0 + gq, :]
        kw = k_ref[k0:k0 + wk, :]
        vw = v_ref[k0:k0 + wk, :]
        s_loc = lax.dot_general(q, kw, nt_dims, preferred_element_type=F32) + tab_ref[kind]
        s_ctx = lax.dot_general(q, kc, nt_dims, preferred_element_type=F32)
        m = jnp.maximum(jnp.max(s_loc, axis=1, keepdims=True), jnp.max(s_ctx, axis=1, keepdims=True))
        p_loc = jnp.exp(s_loc - m)
        p_ctx = jnp.exp(s_ctx - m)
        den = jnp.sum(p_loc, axis=1, keepdims=True) + jnp.sum(p_ctx, axis=1, keepdims=True)
        o = (jnp.dot(p_loc.astype(BF16), vw, preferred_element_type=F32)
             + jnp.dot(p_ctx.astype(BF16), vc, preferred_element_type=F32))
        o_ref[q0:q0 + gq, :] = (o / den).astype(o_ref.dtype)


def _attn_bias_tables(rpb, n_rows):
    n_groups = n_rows // ATTN_Q_ROWS
    kr_n = min(WIN_ROWS, n_rows)

    def row_pattern(gi):
        ws = _attn_window_start(gi, n_rows)
        r = gi * ATTN_Q_ROWS + np.arange(ATTN_Q_ROWS)[:, None]
        kr = ws + np.arange(ATTN_K_ROWS)[None, :]
        r0 = np.clip(r - kr_n // 2, 0, n_rows - kr_n)
        valid = (kr >= r0) & (kr < r0 + kr_n)
        return valid, np.clip(kr - r + WIN_ROWS - 1, 0, 2 * WIN_ROWS - 2)

    pats = [row_pattern(g) for g in range(n_groups)]
    for g in range(2, n_groups - 1):
        assert all(np.array_equal(a, b) for a, b in zip(pats[1], pats[g]))
    kinds = (0, 1, n_groups - 1)
    valid_r = np.stack([pats[g][0] for g in kinds])
    dr = np.stack([pats[g][1] for g in kinds])
    qc = np.arange(GRID_W)[:, None]
    kc = np.arange(GRID_W)[None, :]
    c0 = np.clip(qc - WIN_COLS // 2, 0, GRID_W - WIN_COLS)
    valid_c = (kc >= c0) & (kc < c0 + WIN_COLS)
    dc = np.clip(kc - qc + WIN_COLS - 1, 0, 2 * WIN_COLS - 2)
    rsel = jnp.asarray(np.eye(2 * WIN_ROWS - 1, dtype=np.float32)[dr])
    csel = jnp.asarray(np.eye(2 * WIN_COLS - 1, dtype=np.float32)[dc])
    a = jnp.einsum("tqkr,lhrc->lhtqkc", rsel, rpb.astype(F32), precision=lax.Precision.HIGHEST)
    t = jnp.einsum("lhtqkc,pjc->lhtqpkj", a, csel, precision=lax.Precision.HIGHEST)
    valid = valid_r[:, :, None, :, None] & valid_c[None, None, :, None, :]
    t = jnp.where(jnp.asarray(valid)[None, None], t, MASK_VALUE)
    return t.reshape(t.shape[:3] + (ATTN_Q_ROWS * GRID_W, ATTN_K_ROWS * GRID_W))


def _attention(qkvu, tables, layer, n_batch, seq, ctx_len):
    attn_w = N_HEADS * HEAD_DIM
    n_rows_out = n_batch * seq
    ctx_blk0 = (n_batch * seq) // ctx_len
    kern = functools.partial(_attn_kernel, n_rows=seq // GRID_W)
    tab_shape = tables.shape[2:]
    return pl.pallas_call(
        kern,
        grid=(n_batch, N_HEADS),
        in_specs=[
            pl.BlockSpec((seq, HEAD_DIM), lambda b, h: (b, h)),
            pl.BlockSpec((seq, HEAD_DIM), lambda b, h: (b, N_HEADS + h)),
            pl.BlockSpec((seq, HEAD_DIM), lambda b, h: (b, 2 * N_HEADS + h)),
            pl.BlockSpec((ctx_len, HEAD_DIM), lambda b, h: (ctx_blk0 + b, N_HEADS + h)),
            pl.BlockSpec((ctx_len, HEAD_DIM), lambda b, h: (ctx_blk0 + b, 2 * N_HEADS + h)),
            pl.BlockSpec((None, None) + tab_shape, lambda b, h: (layer, h, 0, 0, 0)),
        ],
        out_specs=pl.BlockSpec((seq, HEAD_DIM), lambda b, h: (b, h)),
        out_shape=jax.ShapeDtypeStruct((n_rows_out, attn_w), BF16),
        compiler_params=_compiler_params(("arbitrary", "arbitrary")),
        name="neighbourhood_attention",
    )(qkvu, qkvu, qkvu, qkvu, qkvu, tables)


def _ctx_attn_kernel(q_ref, k_ref, v_ref, o_ref):
    s = lax.dot_general(q_ref[...], k_ref[...], (((1,), (1,)), ((), ())), preferred_element_type=F32)
    m = jnp.max(s, axis=1, keepdims=True)
    p = jnp.exp(s - m)
    den = jnp.sum(p, axis=1, keepdims=True)
    o = jnp.dot(p.astype(BF16), v_ref[...], preferred_element_type=F32)
    o_ref[...] = (o / den).astype(o_ref.dtype)


def _context_attention(qkvu, n_batch, seq, ctx_len):
    blk0 = (n_batch * seq) // ctx_len
    return pl.pallas_call(
        _ctx_attn_kernel,
        grid=(n_batch, N_HEADS),
        in_specs=[
            pl.BlockSpec((ctx_len, HEAD_DIM), lambda b, h: (blk0 + b, h)),
            pl.BlockSpec((ctx_len, HEAD_DIM), lambda b, h: (blk0 + b, N_HEADS + h)),
            pl.BlockSpec((ctx_len, HEAD_DIM), lambda b, h: (blk0 + b, 2 * N_HEADS + h)),
        ],
        out_specs=pl.BlockSpec((ctx_len, HEAD_DIM), lambda b, h: (b, h)),
        out_shape=jax.ShapeDtypeStruct((n_batch * ctx_len, N_HEADS * HEAD_DIM), BF16),
        compiler_params=_compiler_params(("arbitrary", "arbitrary")),
        name="context_attention",
    )(qkvu, qkvu, qkvu)


def _pool_plan(seq):
    t = POOL_TILE
    win = min(2 * t, seq)
    n_tiles = seq // t
    starts = [int(np.clip(j * t - (win - t) // 2, 0, seq - win)) for j in range(n_tiles)]

    def band(w, j):
        left, right = w // 2, w - 1 - w // 2
        pos = j * t + np.arange(t)[:, None]
        src = starts[j] + np.arange(win)[None, :]
        return ((src >= np.maximum(pos - left, 0)) & (src < np.minimum(pos + right + 1, seq))).astype(np.float32)

    kind_tiles = sorted({0, min(1, n_tiles - 1), n_tiles - 1})
    kind_of = [0 if j == 0 else (len(kind_tiles) - 1 if j == n_tiles - 1 else 1) for j in range(n_tiles)]
    for w in POOL_WINDOWS:
        for j in range(n_tiles):
            assert np.array_equal(band(w, kind_tiles[kind_of[j]]), band(w, j))
    bands = np.stack([np.stack([band(w, j) for j in kind_tiles]) for w in POOL_WINDOWS])
    return jnp.asarray(bands, BF16), win, list(zip(starts, kind_of))


def _pool_kernel(u_ref, band_ref, w_ref, sc_ref, o_ref, *, seq, win, tiles):
    t = POOL_TILE
    left = jnp.left_shift(1, pl.program_id(1))
    w = w_ref[...]
    sc = sc_ref[...]
    for j, (ws, kind) in enumerate(tiles):
        t0 = j * t
        total = jnp.dot(band_ref[kind], u_ref[ws:ws + win, :], preferred_element_type=F32)
        pos = t0 + lax.broadcasted_iota(jnp.int32, total.shape, 0)
        cnt = jnp.minimum(pos + left, seq) - jnp.maximum(pos - left, 0)
        pooled = (total / cnt.astype(F32) - u_ref[t0:t0 + t, :].astype(F32)).astype(BF16)
        y = jnp.dot(pooled, w, preferred_element_type=F32)
        o_ref[t0:t0 + t, :] = (y * sc).astype(o_ref.dtype)


def _pool(qkvu, w_pool_bf16, layer, pool_scale, n_seqs, seq, row0):
    n_g = len(POOL_WINDOWS)
    assert POOL_WINDOWS == tuple(2 << g for g in range(n_g))
    cg = w_pool_bf16.shape[-1]
    u_col0 = (3 * N_HEADS * HEAD_DIM) // cg
    bands, win, tiles = _pool_plan(seq)
    seq_blk0 = row0 // seq
    return pl.pallas_call(
        functools.partial(_pool_kernel, seq=seq, win=win, tiles=tiles),
        grid=(n_seqs, n_g),
        in_specs=[
            pl.BlockSpec((seq, cg), lambda b, g: (seq_blk0 + b, u_col0 + g)),
            pl.BlockSpec((None,) + bands.shape[1:], lambda b, g: (g, 0, 0, 0)),
            pl.BlockSpec((None, None, cg, cg), lambda b, g: (layer, g, 0, 0)),
            pl.BlockSpec((1, cg), lambda b, g: (0, g)),
        ],
        out_specs=pl.BlockSpec((seq, cg), lambda b, g: (b, g)),
        out_shape=jax.ShapeDtypeStruct((n_seqs * seq, n_g * cg), BF16),
        compiler_params=_compiler_params(("arbitrary", "arbitrary")),
        name="multiscale_pool",
    )(qkvu, bands, w_pool_bf16, pool_scale.reshape(1, n_g * cg))


def _first_max_index(vals):
    best = vals[0]
    for v in vals[1:]:
        best = jnp.maximum(best, v)
    idx = jnp.full(best.shape, len(vals), jnp.int32)
    for j in reversed(range(len(vals))):
        idx = jnp.where(vals[j] == best, j, idx)
    return best, idx


def _route(hb, wr_ref, rb_ref):
    logits = jnp.dot(hb, wr_ref[...], preferred_element_type=F32)
    s = jax.nn.sigmoid(logits)
    sb = s + rb_ref[...]
    lane = lax.broadcasted_iota(jnp.int32, sb.shape, 1)
    n_g = N_EXPERT_GROUPS

    def member(arr, j):
        return arr if j == 0 else pltpu.roll(arr, V7X_LANES - n_g * j, axis=1)

    sb_j = [member(sb, j) for j in range(EXPERTS_PER_GROUP)]
    s_j = [member(s, j) for j in range(EXPERTS_PER_GROUP)]
    hi1, lo1 = jnp.maximum(sb_j[0], sb_j[1]), jnp.minimum(sb_j[0], sb_j[1])
    hi2, lo2 = jnp.maximum(sb_j[2], sb_j[3]), jnp.minimum(sb_j[2], sb_j[3])
    top1 = jnp.maximum(hi1, hi2)
    top2 = jnp.maximum(jnp.minimum(hi1, hi2), jnp.maximum(lo1, lo2))
    grp = jnp.where(lane < n_g, top1 + top2, -jnp.inf)
    gmax = jnp.max(grp, axis=1, keepdims=True)
    lane_f = lane.astype(F32)
    gidx = jnp.min(jnp.where(grp == gmax, lane_f, float(V7X_LANES)), axis=1, keepdims=True)
    sel = lane_f == gidx
    sbv = [jnp.sum(jnp.where(sel, v, 0.0), axis=1, keepdims=True) for v in sb_j]
    sv = [jnp.sum(jnp.where(sel, v, 0.0), axis=1, keepdims=True) for v in s_j]
    _, loc0 = _first_max_index(sbv)
    _, loc1 = _first_max_index([jnp.where(loc0 == j, -jnp.inf, sbv[j]) for j in range(EXPERTS_PER_GROUP)])
    pick = lambda loc: sum(jnp.where(loc == j, sv[j], 0.0) for j in range(EXPERTS_PER_GROUP))
    s0, s1 = pick(loc0), pick(loc1)
    tot = s0 + s1
    e0 = gidx * EXPERTS_PER_GROUP + loc0.astype(F32)
    e1 = gidx * EXPERTS_PER_GROUP + loc1.astype(F32)
    return jnp.where(lane == 0, e0, jnp.where(lane == 1, e1,
                     jnp.where(lane == 2, s0 / tot, jnp.where(lane == 3, s1 / tot, 0.0))))


def _out_router_kernel(*refs, n_lat_tiles, split):
    refs = list(refs)

    def rows(is_split):
        if not is_split:
            return refs.pop(0)[...]
        lat, ctx = refs.pop(0), refs.pop(0)
        return jnp.where(pl.program_id(0) < n_lat_tiles, lat[...], ctx[...])

    a, p, x = (rows(s) for s in split)
    mod_ref, g_ref, wo_ref, wr_ref, rb_ref, xo_ref, h_ref, r_ref = refs
    aw = a.shape[1]
    mix = (jnp.dot(a, wo_ref[0:aw, :], preferred_element_type=F32)
           + jnp.dot(p, wo_ref[aw:, :], preferred_element_type=F32))
    x = x + mod_ref[2:3, :] * mix
    xo_ref[...] = x
    hb = _rms_modulate(x, g_ref[...], mod_ref[3:4, :], mod_ref[4:5, :]).astype(BF16)
    h_ref[...] = _pack_bf16_pairs(hb)
    r_ref[...] = _route(hb, wr_ref, rb_ref)


def _out_router(attn, pool, xs, mod_all, layer, gain, w_out_bf16, w_router_perm, rb_perm, n_rows,
                n_lat_tiles, mod_row):
    tm = TOKEN_TILE
    row = lambda w: pl.BlockSpec((tm, w), lambda i: (i, 0))
    operands, specs, split, widths = [], [], [], []
    for op in (attn, pool, xs):
        split.append(isinstance(op, tuple))
        parts = op if split[-1] else (op,)
        widths.append(parts[0].shape[1])
        operands += parts
        specs += list(_split_rows_specs(tm, widths[-1], n_lat_tiles)) if split[-1] else [row(widths[-1])]
    aw, pw, d = widths
    return pl.pallas_call(
        functools.partial(_out_router_kernel, n_lat_tiles=n_lat_tiles, split=tuple(split)),
        grid=(n_rows // tm,),
        in_specs=[
            *specs,
            pl.BlockSpec((None, None, V7X_SUBLANES, d), lambda i: (layer, mod_row(i), 0, 0)),
            pl.BlockSpec((1, d), lambda i: (0, 0)),
            pl.BlockSpec((None, aw + pw, d), lambda i: (layer, 0, 0), pipeline_mode=pl.Buffered(1)),
            pl.BlockSpec((d, V7X_LANES), lambda i: (0, 0)),
            pl.BlockSpec((1, V7X_LANES), lambda i: (0, 0)),
        ],
        out_specs=[row(d), row(d // 2), row(V7X_LANES)],
        out_shape=[jax.ShapeDtypeStruct((n_rows, d), F32),
                   jax.ShapeDtypeStruct((n_rows, d // 2), jnp.uint32),
                   jax.ShapeDtypeStruct((n_rows, V7X_LANES), F32)],
        compiler_params=_compiler_params(("arbitrary",)),
        name="out_proj_router",
    )(*operands, mod_all, gain.reshape(1, d), w_out_bf16, w_router_perm, rb_perm)


def _pack_bf16_pairs(hb):
    k = hb.shape[1] // 2
    hi = lax.bitcast_convert_type(hb[:, :k].astype(F32), jnp.uint32)
    lo = lax.bitcast_convert_type(hb[:, k:].astype(F32), jnp.uint32)
    return hi | (lo >> 16)


def _unpack_bf16_pairs(w):
    hi = lax.bitcast_convert_type(w & jnp.uint32(0xFFFF0000), F32).astype(BF16)
    lo = lax.bitcast_convert_type(w << 16, F32).astype(BF16)
    return hi, lo


def _moe_kernel(blk_e_ref, n_used_ref, src_ref, h_hbm, wg_ref, wu_ref, wd_ref, o_ref,
                xbuf0, xbuf1, sem0, sem1):
    del blk_e_ref
    i = pl.program_id(0)
    n_used = n_used_ref[0]
    bm = EXPERT_BLOCK
    slots = ((xbuf0, sem0), (xbuf1, sem1))

    def start_gather(blk, slot):
        buf, sem = slots[slot]
        for r in range(bm):
            pltpu.make_async_copy(h_hbm.at[pl.ds(src_ref[blk * bm + r], 1), :],
                                  buf.at[pl.ds(r, 1), :], sem).start(priority=r % 2)

    def wait_gather(slot):
        buf, sem = slots[slot]
        pltpu.make_async_copy(h_hbm.at[pl.ds(0, bm), :], buf, sem).wait()

    @pl.when(i == 0)
    def _():
        start_gather(0, 0)

    for parity in (0, 1):
        @pl.when((i % 2 == parity) & (i == n_used))
        def _():
            wait_gather(parity)

        @pl.when((i % 2 == parity) & (i < n_used))
        def _():
            wait_gather(parity)
            start_gather(i + 1, 1 - parity)
            _expert_block(slots[parity][0], wg_ref, wu_ref, wd_ref, o_ref)

    @pl.when(i >= n_used)
    def _():
        o_ref[...] = jnp.zeros(o_ref.shape, o_ref.dtype)


def _expert_block(x_ref, wg_ref, wu_ref, wd_ref, o_ref):
    xa, xb = _unpack_bf16_pairs(x_ref[...])
    k = xa.shape[1]
    ff = wg_ref.shape[1]
    acc = None
    for f0 in range(0, ff, FF_CHUNK):
        cols = slice(f0, f0 + FF_CHUNK)
        g = (jnp.dot(xa, wg_ref[:k, cols], preferred_element_type=F32)
             + jnp.dot(xb, wg_ref[k:, cols], preferred_element_type=F32))
        u = (jnp.dot(xa, wu_ref[:k, cols], preferred_element_type=F32)
             + jnp.dot(xb, wu_ref[k:, cols], preferred_element_type=F32))
        a = ((g * jax.nn.sigmoid(g)) * u).astype(BF16)
        part = jnp.dot(a, wd_ref[cols, :], preferred_element_type=F32)
        acc = part if acc is None else acc + part
    o_ref[...] = acc.astype(o_ref.dtype)


def _expert_mlp(h2p, row_src, blk_e, n_used, wg, wu, wd, layer):
    bm = EXPERT_BLOCK
    n_blk = row_src.shape[0] // bm - 1
    d, ff = wg.shape[-2:]
    w_in_spec = pl.BlockSpec((None, None, d, ff), lambda i, be, nu, src: (layer, be[i], 0, 0))
    grid_spec = pltpu.PrefetchScalarGridSpec(
        num_scalar_prefetch=3,
        grid=(n_blk,),
        in_specs=[
            pl.BlockSpec(memory_space=pl.ANY),
            w_in_spec, w_in_spec,
            pl.BlockSpec((None, None, ff, d), lambda i, be, nu, src: (layer, be[i], 0, 0)),
        ],
        out_specs=pl.BlockSpec((bm, d), lambda i, be, nu, src: (i, 0)),
        scratch_shapes=[pltpu.VMEM((bm, h2p.shape[1]), h2p.dtype)] * 2 + [pltpu.SemaphoreType.DMA(())] * 2,
    )
    return pl.pallas_call(
        _moe_kernel, grid_spec=grid_spec,
        out_shape=jax.ShapeDtypeStruct((n_blk * bm, d), BF16),
        compiler_params=_compiler_params(("arbitrary",)),
        name="expert_mlp",
    )(blk_e, n_used, row_src, h2p, wg, wu, wd)


def _dispatch_plan(expert, n_tokens):
    bm = EXPERT_BLOCK
    a = n_tokens * TOP_K
    n_blk = (a + N_EXPERTS * (bm - 1)) // bm + 1
    flat_e = expert.reshape(-1)
    onehot = (flat_e[:, None] == jnp.arange(N_EXPERTS, dtype=jnp.int32)[None, :]).astype(jnp.int32)
    csum = jnp.cumsum(onehot, axis=0)
    rank = jnp.take_along_axis(csum, flat_e[:, None], axis=1)[:, 0] - 1
    counts = csum[-1]
    padded = (counts + bm - 1) // bm * bm
    pend = jnp.cumsum(padded)
    dest = (pend - padded)[flat_e] + rank
    row_src = jnp.zeros(((n_blk + 1) * bm,), jnp.int32).at[dest].set(
        jnp.arange(a, dtype=jnp.int32) // TOP_K, unique_indices=True, mode="promise_in_bounds")
    blk_start = jnp.arange(n_blk, dtype=jnp.int32) * bm
    blk_e = jnp.minimum(jnp.sum(blk_start[:, None] >= pend[None, :], axis=1), N_EXPERTS - 1).astype(jnp.int32)
    n_used = (pend[-1] // bm).astype(jnp.int32).reshape(1)
    return dest.reshape(n_tokens, TOP_K), row_src, blk_e, n_used


def _moe(h2p, route, wg, wu, wd, layer):
    n_tokens = h2p.shape[0]
    expert = route[:, :TOP_K].astype(jnp.int32)
    dest, row_src, blk_e, n_used = _dispatch_plan(expert, n_tokens)
    yb = _expert_mlp(h2p, row_src, blk_e, n_used, wg, wu, wd, layer)
    return tuple(yb.at[dest[:, k]].get(mode="promise_in_bounds") for k in range(TOP_K))


def _final_kernel(x_ref, y0_ref, y1_ref, gate_ref, mod_ref, g_ref, o_ref):
    x = x_ref[...] + mod_ref[5:6, :] * _combine(y0_ref, y1_ref, gate_ref)
    o_ref[...] = (x * lax.rsqrt(jnp.mean(x * x, axis=-1, keepdims=True) + EPS)) * g_ref[...]


def _final(x, y2, route, mod_all, layer, gain, n_rows, mod_row):
    d = x.shape[1]
    tm = TOKEN_TILE
    row_spec = pl.BlockSpec((tm, d), lambda i: (i, 0))
    return pl.pallas_call(
        _final_kernel,
        grid=(n_rows // tm,),
        in_specs=[row_spec, row_spec, row_spec,
                  pl.BlockSpec((tm, V7X_LANES), lambda i: (i, 0)),
                  pl.BlockSpec((None, None, V7X_SUBLANES, d), lambda i: (layer, mod_row(i), 0, 0)),
                  pl.BlockSpec((1, d), lambda i: (0, 0))],
        out_specs=row_spec,
        out_shape=jax.ShapeDtypeStruct((n_rows, d), F32),
        compiler_params=_compiler_params(("arbitrary",)),
        name="final_norm",
    )(x, *y2, route, mod_all, gain.reshape(1, d))


def kernel(x, c, ctx, c_ctx, w_mod, b_mod, norm1, norm2, w_in, rpb, w_pool, pool_scale, w_out,
           w_router, router_bias, w_gate, w_up, w_down, norm_final):
    n_batch, seq, d = x.shape
    ctx_len = ctx.shape[1]
    depth = w_mod.shape[0]
    n_lat = n_batch * seq
    n_ctx = n_batch * ctx_len
    nt = n_lat + n_ctx
    tm = TOKEN_TILE
    assert seq % tm == 0 and n_ctx % tm == 0 and seq % (GRID_W * ATTN_Q_ROWS) == 0
    assert seq % ctx_len == 0 and ctx_len % POOL_TILE == 0 and d == 2 * N_HEADS * HEAD_DIM

    mod_rows = _round_up(n_batch + 1, V7X_SUBLANES)
    cc = jnp.zeros((mod_rows, d), F32).at[:n_batch].set(c).at[n_batch].set(c_ctx)
    mod_all = _modulation(cc, w_mod, b_mod).reshape(depth, mod_rows, N_MOD, d)
    mod_all = jnp.pad(mod_all, ((0, 0), (0, 0), (0, V7X_SUBLANES - N_MOD), (0, 0)))
    n_lat_tiles = n_lat // tm
    mod_row = lambda i: jnp.where(i < n_lat_tiles, i // (seq // tm), n_batch)

    lanes = np.arange(N_EXPERTS)
    perm = (lanes % N_EXPERT_GROUPS) * EXPERTS_PER_GROUP + lanes // N_EXPERT_GROUPS
    w_router_perm = jnp.zeros((d, V7X_LANES), BF16).at[:, :N_EXPERTS].set(w_router[:, perm].astype(BF16))
    rb_perm = jnp.zeros((1, V7X_LANES), F32).at[0, :N_EXPERTS].set(router_bias[perm].astype(F32))

    w_in_b, w_out_b, w_pool_b = w_in.astype(BF16), w_out.astype(BF16), w_pool.astype(BF16)
    w_gate_b, w_up_b, w_down_b = w_gate.astype(BF16), w_up.astype(BF16), w_down.astype(BF16)
    tables = _attn_bias_tables(rpb, seq // GRID_W)

    xs = (x.reshape(n_lat, d), ctx.reshape(n_ctx, d))
    y2 = route = None
    for l in range(depth):
        last = l == depth - 1
        n_rows = n_lat if last else nt
        if l == 0:
            qkvu = _first_proj(xs[0], xs[1], mod_all, l, norm1[l], w_in_b, mod_row)
        else:
            xs, qkvu = _next_proj(xs, y2, route, mod_all, l, norm1[l], w_in_b, mod_row)
        attn = _attention(qkvu, tables, l, n_batch, seq, ctx_len)
        pool = _pool(qkvu, w_pool_b, l, pool_scale[l], n_batch, seq, 0)
        if not last:
            attn = (attn, _context_attention(qkvu, n_batch, seq, ctx_len))
            pool = (pool, _pool(qkvu, w_pool_b, l, pool_scale[l], n_batch, ctx_len, n_lat))
        xs, h2, route = _out_router(attn, pool, xs, mod_all, l, norm2[l], w_out_b,
                                    w_router_perm, rb_perm, n_rows, n_lat_tiles, mod_row)
        y2 = _moe(h2, route, w_gate_b, w_up_b, w_down_b, l)
    out = _final(xs, y2, route, mod_all, depth - 1, norm_final, n_lat, mod_row)
    return out.reshape(n_batch, seq, d)
```
